```python
import math
import jax
import jax.numpy as jnp
from jax import lax
import numpy as np

D_MODEL = 1024
BATCH = 4
SEQ = 8192
DEPTH = 2
DEC_BATCH = 32
DEC_SEQ = 4
PAST_LEN = 16384
PAGE_SIZE = 128

HEAD_DIM = 64
ROPE_DIM = HEAD_DIM // 4
ROPE_THETA = 500000.0
Q_BLOCK = 128
NORM_EPS = 1e-6
BRANCH_WIDTH = D_MODEL // 2
N_BRANCH = 4

NSA_HEADS = BRANCH_WIDTH // HEAD_DIM
NSA_KV_HEADS = NSA_HEADS // 4
NSA_GROUP = NSA_HEADS // NSA_KV_HEADS
NSA_CMP_LEN = 32
NSA_CMP_STRIDE = 16
NSA_CMP_HIDDEN = 2 * HEAD_DIM
NSA_SEL_LEN = 64
NSA_TOP_N = 16
NSA_WINDOW = 512

SSD_HEAD_DIM = 64
SSD_HEADS = BRANCH_WIDTH // SSD_HEAD_DIM
SSD_INNER = SSD_HEADS * SSD_HEAD_DIM
SSD_GROUPS = 2
SSD_STATE = 128
SSD_CONV = 4
SSD_CONV_DIM = SSD_INNER + 2 * SSD_GROUPS * SSD_STATE
SSD_CHUNK = 128

SB_HEADS = BRANCH_WIDTH // HEAD_DIM
SB_W = SB_HEADS * HEAD_DIM

CONF_CH = BRANCH_WIDTH
CONF_WIDTH = 31

D_FF = 11 * D_MODEL // 4
N_EXPERTS = 8
TOP_K = 2
D_FF_EXPERT = 7 * D_MODEL // 2
MOE_BLOCK = 128
N_DENSE = (DEPTH + 1) // 2
N_MOE = DEPTH // 2

NSA_Q = NSA_HEADS * HEAD_DIM
NSA_KV = NSA_KV_HEADS * HEAD_DIM
IN_SPLITS = (NSA_Q, 6 * NSA_KV, 3 * NSA_HEADS, SSD_INNER, SSD_CONV_DIM, SSD_HEADS, 3 * SB_W, 2 * CONF_CH, N_BRANCH * D_MODEL)
IN_COLS = sum(IN_SPLITS)

kernel_name = 'hybrid_nsa_ssd_stickbreak_conformer_step'


def rms_norm(x, g, eps=NORM_EPS):
    xf = x.astype(jnp.float32)
    y = xf * lax.rsqrt(jnp.mean(xf * xf, axis=-1, keepdims=True) + eps)
    return (y * g.astype(jnp.float32)).astype(x.dtype)


def layer_norm(x, g, b, eps=1e-5):
    xf = x.astype(jnp.float32)
    mu = jnp.mean(xf, axis=-1, keepdims=True)
    var = jnp.mean(jnp.square(xf - mu), axis=-1, keepdims=True)
    return ((xf - mu) * lax.rsqrt(var + eps) * g.astype(jnp.float32) + b.astype(jnp.float32)).astype(x.dtype)


def partial_rope(x, pos):
    half = ROPE_DIM // 2
    inv = ROPE_THETA ** (-jnp.arange(half, dtype=jnp.float32) / half)
    ang = pos.astype(jnp.float32)[:, None] * inv[None, :]
    ang = ang.reshape((ang.shape[0],) + (1,) * (x.ndim - 3) + (half,))
    cos, sin = jnp.cos(ang), jnp.sin(ang)
    xr = x[..., :ROPE_DIM].astype(jnp.float32)
    x1, x2 = xr[..., :half], xr[..., half:]
    rot = jnp.concatenate([x1 * cos - x2 * sin, x2 * cos + x1 * sin], axis=-1)
    return jnp.concatenate([rot.astype(x.dtype), x[..., ROPE_DIM:]], axis=-1)


def masked_softmax(s, mask, axis=-1):
    s = jnp.where(mask, s.astype(jnp.float32), -jnp.inf)
    m = jnp.max(s, axis=axis, keepdims=True)
    m = jnp.where(jnp.isfinite(m), m, 0.0)
    e = jnp.exp(s - m)
    den = jnp.sum(e, axis=axis, keepdims=True)
    return e / jnp.where(den > 0, den, 1.0)


def causal_dwconv(x, prev, w, b):
    xp = jnp.concatenate([prev.astype(x.dtype), x], axis=1)
    y = lax.conv_general_dilated(xp, w[:, None, :].astype(x.dtype), (1,), 'VALID',
                                 dimension_numbers=('NWC', 'WIO', 'NWC'), feature_group_count=x.shape[-1])
    return y + b, xp[:, xp.shape[1] - (w.shape[0] - 1):]


def gather_pages(pool, layer, kind, page_table):
    rows = pool[layer, page_table, :, kind]
    return rows.reshape((rows.shape[0], -1) + rows.shape[3:])


def nsa_compress(k, pe, w1, w2):
    B, T, H, d = k.shape
    S = NSA_CMP_STRIDE
    R = NSA_CMP_LEN // S
    n_chunk = -(-T // S)
    k = jnp.pad(k, ((0, 0), (0, n_chunk * S - T), (0, 0), (0, 0)))
    chunks = k.reshape(B, n_chunk, S, H, d).transpose(0, 1, 3, 2, 4).reshape(B, n_chunk, H, S * d)
    proj = jnp.einsum('bnhf,rfc->rbnhc', chunks, w1.reshape(R, S * d, -1))
    n_cmp = n_chunk - R + 1
    hid = pe.reshape(-1) @ w1
    for r in range(R):
        hid = hid + proj[r][:, r:r + n_cmp]
    kc = jax.nn.gelu(hid) @ w2
    cmp_end = jnp.arange(n_cmp) * S + NSA_CMP_LEN - 1
    return kc, cmp_end


def to_sel_blocks(k):
    B, T, H, d = k.shape
    n_sel = -(-T // NSA_SEL_LEN)
    k = jnp.pad(k, ((0, 0), (0, n_sel * NSA_SEL_LEN - T), (0, 0), (0, 0)))
    return k.reshape(B, n_sel, NSA_SEL_LEN, H, d).transpose(0, 3, 1, 2, 4)


def nsa_block(q, qpos, gates, kc, vc, cmp_end, ksb, vsb, kw, vw, kwpos):
    B, Tq = q.shape[:2]
    scale = HEAD_DIM ** -0.5
    qg = q.reshape(B, Tq, NSA_KV_HEADS, NSA_GROUP, HEAD_DIM)
    s_c = jnp.einsum('bqhgd,bchd->bhgqc', qg, kc) * scale
    p_c = masked_softmax(s_c, cmp_end[None, :] <= qpos[:, None])
    o_c = jnp.einsum('bhgqc,bchd->bqhgd', p_c.astype(vc.dtype), vc)
    n_sel = ksb.shape[2]
    i_start = jnp.arange(kc.shape[1]) * NSA_CMP_STRIDE
    j_start = jnp.arange(n_sel) * NSA_SEL_LEN
    cover = ((i_start[:, None] < j_start[None, :] + NSA_SEL_LEN)
             & (i_start[:, None] + NSA_CMP_LEN > j_start[None, :])).astype(jnp.float32)
    imp = jnp.einsum('bhgqc,cj->bhqj', p_c, cover)
    cur = qpos // NSA_SEL_LEN
    blk = jnp.arange(n_sel)[None, :]
    forced = (blk == 0) | (blk == cur[:, None]) | (blk == cur[:, None] - 1)
    valid = j_start[None, :] <= qpos[:, None]
    imp = jnp.where(forced, jnp.inf, jnp.where(valid, imp, -jnp.inf))
    _, idx = lax.top_k(imp, min(NSA_TOP_N, n_sel))
    bi = jnp.arange(B)[:, None, None, None]
    hi = jnp.arange(NSA_KV_HEADS)[None, :, None, None]
    k_sel = ksb[bi, hi, idx]
    v_sel = vsb[bi, hi, idx]
    s_s = jnp.einsum('bqhgd,bhqnld->bhgqnl', qg, k_sel) * scale
    kpos = idx[..., None] * NSA_SEL_LEN + jnp.arange(NSA_SEL_LEN)
    m_s = (kpos <= qpos[:, None, None])[:, :, None]
    p_s = masked_softmax(s_s, m_s, axis=(-2, -1))
    o_s = jnp.einsum('bhgqnl,bhqnld->bqhgd', p_s.astype(v_sel.dtype), v_sel)
    s_w = jnp.einsum('bqhgd,bkhd->bhgqk', qg, kw) * scale
    dpos = qpos[:, None] - kwpos[None, :]
    m_w = (dpos >= 0) & (dpos < NSA_WINDOW) & (kwpos[None, :] >= 0)
    p_w = masked_softmax(s_w, m_w)
    o_w = jnp.einsum('bhgqk,bkhd->bqhgd', p_w.astype(vw.dtype), vw)
    g = gates.reshape(B, Tq, NSA_KV_HEADS, NSA_GROUP, 3)
    o = g[..., 0:1] * o_c + g[..., 1:2] * o_s + g[..., 2:3] * o_w
    return o.reshape(B, Tq, NSA_Q)


def nsa_mixer(q_raw, kv_raw, gate_raw, pos, q_gain, k_gain, cmp_pe, cmp_w1, cmp_w2, past):
    B, T, _ = q_raw.shape
    q = partial_rope(rms_norm(q_raw.reshape(B, T, NSA_HEADS, HEAD_DIM), q_gain), pos)
    kv = kv_raw.reshape(B, T, 6, NSA_KV_HEADS, HEAD_DIM)
    kn = partial_rope(rms_norm(kv[:, :, 0::2], k_gain[:, None, :]), pos)
    new_rows = [kn[:, :, 0], kv[:, :, 1], kn[:, :, 1], kv[:, :, 3]]
    win_new = jnp.stack([kn[:, :, 2], kv[:, :, 5]], axis=2)
    gates = jax.nn.sigmoid(gate_raw.astype(jnp.float32)).astype(q.dtype).reshape(B, T, NSA_HEADS, 3)
    if past is None:
        full = new_rows
    else:
        past_rows, past_win = past
        full = [jnp.concatenate([p, n], axis=1) for p, n in zip(past_rows, new_rows)]
    kc, cmp_end = nsa_compress(full[0], cmp_pe[0], cmp_w1[0], cmp_w2[0])
    vc, _ = nsa_compress(full[1], cmp_pe[1], cmp_w1[1], cmp_w2[1])
    ksb, vsb = to_sel_blocks(full[2]), to_sel_blocks(full[3])
    if past is None:
        wpad = jnp.pad(win_new, ((0, 0), (NSA_WINDOW, 0), (0, 0), (0, 0), (0, 0)))

        def one_block(b):
            q0 = b * Q_BLOCK
            wb = lax.dynamic_slice_in_dim(wpad, q0, NSA_WINDOW + Q_BLOCK, axis=1)
            return nsa_block(lax.dynamic_slice_in_dim(q, q0, Q_BLOCK, axis=1), q0 + jnp.arange(Q_BLOCK),
                             lax.dynamic_slice_in_dim(gates, q0, Q_BLOCK, axis=1), kc, vc, cmp_end, ksb, vsb,
                             wb[:, :, 0], wb[:, :, 1], q0 - NSA_WINDOW + jnp.arange(NSA_WINDOW + Q_BLOCK))

        o = lax.map(one_block, jnp.arange(T // Q_BLOCK))
        o = jnp.swapaxes(o, 0, 1).reshape(B, T, NSA_Q)
        win_state = win_new[:, T - min(NSA_WINDOW, T):]
    else:
        w_all = jnp.concatenate([past_win, win_new], axis=1)
        start = full[0].shape[1] - w_all.shape[1]
        o = nsa_block(q, pos, gates, kc, vc, cmp_end, ksb, vsb, w_all[:, :, 0], w_all[:, :, 1],
                      start + jnp.arange(w_all.shape[1]))
        win_state = w_all[:, T:]
    return o, jnp.stack(new_rows, axis=2), win_state


def ssd_chunked(x, dt, a, bm, cm, h0, chunk):
    B, T, H, P = x.shape
    nc = T // chunk
    rep = H // bm.shape[2]
    bh = jnp.repeat(bm.astype(jnp.float32), rep, axis=2).reshape(B, nc, chunk, H, -1)
    ch = jnp.repeat(cm.astype(jnp.float32), rep, axis=2).reshape(B, nc, chunk, H, -1)
    xdt = (x.astype(jnp.float32) * dt[..., None]).reshape(B, nc, chunk, H, P)
    la = lax.cumsum((dt * a).reshape(B, nc, chunk, H), axis=2)
    causal = jnp.tril(jnp.ones((chunk, chunk), bool))[None, None, :, :, None]
    seg = la[:, :, :, None, :] - la[:, :, None, :, :]
    decay = jnp.exp(jnp.where(causal, seg, -jnp.inf))
    cb = jnp.einsum('bctHn,bcsHn->bctsH', ch, bh)
    y = jnp.einsum('bctsH,bcsHp->bctHp', cb * decay, xdt)
    to_end = jnp.exp(la[:, :, -1:, :] - la)
    s_chunk = jnp.einsum('bclHn,bclH,bclHp->bcHpn', bh, to_end, xdt)
    c_decay = jnp.exp(la[:, :, -1, :])

    def step(h, inp):
        s_c, dec = inp
        return h * dec[:, :, None, None] + s_c, h

    h_last, h_start = lax.scan(step, h0.astype(jnp.float32), (jnp.swapaxes(s_chunk, 0, 1), jnp.swapaxes(c_decay, 0, 1)))
    h_start = jnp.swapaxes(h_start, 0, 1)
    y = y + jnp.einsum('bclHn,bcHpn->bclHp', ch * jnp.exp(la)[..., None], h_start)
    return y.reshape(B, T, H, P), h_last


def ssd_mixer(z, xbc_raw, dt_raw, conv_w, conv_b, dt_bias, a_log, d_skip, norm_g, past):
    B, T, _ = z.shape
    if past is None:
        h0 = jnp.zeros((B, SSD_HEADS, SSD_HEAD_DIM, SSD_STATE), jnp.float32)
        prev = jnp.zeros((B, SSD_CONV - 1, SSD_CONV_DIM), xbc_raw.dtype)
        chunk = min(SSD_CHUNK, T)
    else:
        h0, prev = past
        chunk = T
    xbc, conv_state = causal_dwconv(xbc_raw, prev, conv_w, conv_b)
    xbc = jax.nn.silu(xbc)
    xs, bm, cm = jnp.split(xbc, [SSD_INNER, SSD_INNER + SSD_GROUPS * SSD_STATE], axis=-1)
    xs = xs.reshape(B, T, SSD_HEADS, SSD_HEAD_DIM)
    bm = bm.reshape(B, T, SSD_GROUPS, SSD_STATE)
    cm = cm.reshape(B, T, SSD_GROUPS, SSD_STATE)
    dt = jax.nn.softplus(dt_raw.astype(jnp.float32) + dt_bias.astype(jnp.float32))
    a = -jnp.exp(a_log.astype(jnp.float32))
    y, h_new = ssd_chunked(xs, dt, a, bm, cm, h0, chunk)
    y = y + d_skip.astype(jnp.float32)[:, None] * xs.astype(jnp.float32)
    y = y * jax.nn.silu(z.astype(jnp.float32)).reshape(B, T, SSD_HEADS, SSD_HEAD_DIM)
    y = rms_norm(y.reshape(B, T, SSD_GROUPS, SSD_INNER // SSD_GROUPS), norm_g.reshape(SSD_GROUPS, -1))
    return y.reshape(B, T, SSD_INNER).astype(z.dtype), h_new, conv_state


def sb_block(q, qpos, k, v, kpos):
    z = jnp.einsum('bqhd,bkhd->bhqk', q, k).astype(jnp.float32) * (HEAD_DIM ** -0.5)
    mask = kpos[None, :] < qpos[:, None]
    log_keep = jnp.where(mask, jax.nn.log_sigmoid(-z), 0.0)
    after = lax.cumsum(log_keep, axis=3, reverse=True) - log_keep
    w = jnp.where(mask, jnp.exp(jax.nn.log_sigmoid(z) + after), 0.0)
    return jnp.einsum('bhqk,bkhd->bqhd', w.astype(v.dtype), v)


def sb_mixer(qkv_raw, pos, past):
    B, T, _ = qkv_raw.shape
    qkv = qkv_raw.reshape(B, T, 3, SB_HEADS, HEAD_DIM)
    q = qkv[:, :, 0]
    rows = qkv[:, :, 1:]
    if past is None:
        k, v = qkv[:, :, 1], qkv[:, :, 2]

        def one_block(b):
            q0 = b * Q_BLOCK
            return sb_block(lax.dynamic_slice_in_dim(q, q0, Q_BLOCK, axis=1), q0 + jnp.arange(Q_BLOCK), k, v, pos)

        o = lax.map(one_block, jnp.arange(T // Q_BLOCK))
        o = jnp.swapaxes(o, 0, 1).reshape(B, T, SB_W)
    else:
        k = jnp.concatenate([past[0], qkv[:, :, 1]], axis=1)
        v = jnp.concatenate([past[1], qkv[:, :, 2]], axis=1)
        o = sb_block(q, pos, k, v, jnp.arange(k.shape[1])).reshape(B, T, SB_W)
    return o, rows


def conformer_conv(glu_raw, glu_b, conv_w, conv_b, ln_g, ln_b, past):
    B, T, _ = glu_raw.shape
    a, g = jnp.split(glu_raw + glu_b, 2, axis=-1)
    u = a * jax.nn.sigmoid(g)
    prev = jnp.zeros((B, CONF_WIDTH - 1, CONF_CH), u.dtype) if past is None else past
    c, conv_state = causal_dwconv(u, prev, conv_w, conv_b)
    return jax.nn.silu(layer_norm(c, ln_g, ln_b)), conv_state


def swiglu(h, w_gate, w_up, w_down):
    return (jax.nn.silu(h @ w_gate) * (h @ w_up)) @ w_down


def moe_swiglu(h, w_router, b_router, w_gate, w_up, w_down):
    shape = h.shape
    x = h.reshape(-1, shape[-1])
    n_tok = x.shape[0]
    logits = (x @ w_router).astype(jnp.float32) + b_router.astype(jnp.float32)
    top_logit, top_e = lax.top_k(logits, TOP_K)
    top_w = jax.nn.softmax(top_logit, axis=-1)
    n_slot = n_tok * TOP_K
    slot_e = top_e.reshape(-1)
    slot_tok = jnp.arange(n_slot) // TOP_K
    order = jnp.argsort(slot_e)
    e_sorted = slot_e[order]
    counts = jax.ops.segment_sum(jnp.ones((n_slot,), jnp.int32), slot_e, num_segments=N_EXPERTS)
    padded = (counts + MOE_BLOCK - 1) // MOE_BLOCK * MOE_BLOCK
    pad_end = jnp.cumsum(padded)
    dest = (pad_end - padded)[e_sorted] + jnp.arange(n_slot) - (jnp.cumsum(counts) - counts)[e_sorted]
    n_blk = -(-n_slot // MOE_BLOCK) + N_EXPERTS
    buf_tok = jnp.full((n_blk * MOE_BLOCK,), n_tok, jnp.int32).at[dest].set(slot_tok[order])
    buf_w = jnp.zeros((n_blk * MOE_BLOCK,), jnp.float32).at[dest].set(top_w.reshape(-1)[order])
    blk_e = jnp.minimum(jnp.sum(pad_end[None, :] <= (jnp.arange(n_blk) * MOE_BLOCK)[:, None], axis=1), N_EXPERTS - 1)
    x_pad = jnp.concatenate([x, jnp.zeros((1, x.shape[1]), x.dtype)], axis=0)
    xb = x_pad[buf_tok].reshape(n_blk, MOE_BLOCK, -1)

    def expert_block(args):
        xe, e = args
        return (jax.nn.silu(xe @ w_gate[e]) * (xe @ w_up[e])) @ w_down[e]

    yb = lax.map(expert_block, (xb, blk_e)).reshape(n_blk * MOE_BLOCK, -1)
    y = jnp.zeros((n_tok + 1, x.shape[1]), jnp.float32).at[buf_tok].add(yb.astype(jnp.float32) * buf_w[:, None])
    return y[:n_tok].astype(h.dtype).reshape(shape)


def trunk_layer(x, pos, l, W, past):
    B, T, _ = x.shape
    h = rms_norm(x, W['g_mix'][l])
    proj = h @ W['w_in'][l]
    (nsa_q, nsa_kv, nsa_g, ssd_z, ssd_xbc, ssd_dt, sb_qkv, conf_in, br_gate) = jnp.split(
        proj, np.cumsum(IN_SPLITS)[:-1].tolist(), axis=-1)
    o_nsa, nsa_rows, nsa_win = nsa_mixer(
        nsa_q, nsa_kv, nsa_g, pos, W['nsa_q_gain'][l], W['nsa_k_gain'][l], W['nsa_cmp_pe'][l],
        W['nsa_cmp_w1'][l], W['nsa_cmp_w2'][l], None if past is None else (past['nsa_kv'], past['nsa_win']))
    o_ssd, ssm, ssd_conv = ssd_mixer(
        ssd_z, ssd_xbc, ssd_dt, W['ssd_conv_w'][l], W['ssd_conv_b'][l], W['ssd_dt_bias'][l], W['ssd_a_log'][l],
        W['ssd_d'][l], W['ssd_norm_g'][l], None if past is None else (past['ssm'], past['ssd_conv']))
    o_sb, sb_rows = sb_mixer(sb_qkv, pos, None if past is None else past['sb_kv'])
    o_conf, conf_conv = conformer_conv(
        conf_in, W['conf_glu_b'][l], W['conf_conv_w'][l], W['conf_conv_b'][l], W['conf_ln_g'][l], W['conf_ln_b'][l],
        None if past is None else past['conf_conv'])
    branches = jnp.stack([o_nsa, o_ssd, o_sb, o_conf], axis=2)
    up = jnp.einsum('btiw,iwd->btid', branches, W['w_branch'][l])
    gate = jax.nn.sigmoid(br_gate.reshape(B, T, N_BRANCH, D_MODEL) + W['b_gate'][l])
    x = x + jnp.sum(gate * up, axis=2) @ W['w_out'][l]
    h2 = rms_norm(x, W['g_ffn'][l])
    i = l // 2
    if l % 2 == 0:
        x = x + swiglu(h2, W['ff_w_gate'][i], W['ff_w_up'][i], W['ff_w_down'][i])
    else:
        x = x + moe_swiglu(h2, W['moe_w_router'][i], W['moe_b_router'][i], W['moe_w_gate'][i],
                           W['moe_w_up'][i], W['moe_w_down'][i])
    return x, (nsa_rows, nsa_win, sb_rows, ssm, ssd_conv, conf_conv)


def setup_inputs(seed: int = 0) -> dict:
    key = jax.random.key(seed)
    keys = iter(jax.random.split(key, 64))

    def nrm(shape, scale):
        return jax.random.normal(next(keys), shape, jnp.float32) * scale

    def gain(shape):
        return 1.0 + nrm(shape, 0.1)

    n_pages = PAST_LEN // PAGE_SIZE
    n_phys = (DEC_BATCH * n_pages * 5) // 4
    win_buf = min(NSA_WINDOW, PAST_LEN)
    perm = jax.random.permutation(next(keys), n_phys)
    page_table = perm[:DEC_BATCH * n_pages].reshape(DEC_BATCH, n_pages).astype(jnp.int32)
    u_dt = jax.random.uniform(next(keys), (DEPTH, SSD_HEADS), jnp.float32)
    dt0 = jnp.exp(u_dt * (math.log(0.1) - math.log(0.001)) + math.log(0.001))
    ssd_dt_bias = dt0 + jnp.log(-jnp.expm1(-dt0))
    ssd_a_log = jnp.log(jax.random.uniform(next(keys), (DEPTH, SSD_HEADS), jnp.float32, 1.0, 16.0))
    return dict(
        x_prompt=nrm((BATCH, SEQ, D_MODEL), 1.0),
        x_sample=nrm((DEC_BATCH, DEC_SEQ, D_MODEL), 1.0),
        cache_nsa_kv=nrm((DEPTH, n_phys, PAGE_SIZE, 4, NSA_KV_HEADS, HEAD_DIM), 1.0),
        cache_nsa_win=nrm((DEPTH, DEC_BATCH, win_buf, 2, NSA_KV_HEADS, HEAD_DIM), 1.0),
        cache_sb_kv=nrm((DEPTH, n_phys, PAGE_SIZE, 2, SB_HEADS, HEAD_DIM), 1.0),
        state_ssm=nrm((DEPTH, DEC_BATCH, SSD_HEADS, SSD_HEAD_DIM, SSD_STATE), 0.1),
        state_ssd_conv=nrm((DEPTH, DEC_BATCH, SSD_CONV - 1, SSD_CONV_DIM), 1.0),
        state_conf_conv=nrm((DEPTH, DEC_BATCH, CONF_WIDTH - 1, CONF_CH), 1.0),
        page_table=page_table,
        g_mix=gain((DEPTH, D_MODEL)),
        w_in=nrm((DEPTH, D_MODEL, IN_COLS), D_MODEL ** -0.5),
        nsa_q_gain=gain((DEPTH, HEAD_DIM)),
        nsa_k_gain=gain((DEPTH, 3, HEAD_DIM)),
        nsa_cmp_pe=nrm((DEPTH, 2, NSA_CMP_LEN, HEAD_DIM), 0.1),
        nsa_cmp_w1=nrm((DEPTH, 2, NSA_CMP_LEN * HEAD_DIM, NSA_CMP_HIDDEN), (NSA_CMP_LEN * HEAD_DIM) ** -0.5),
        nsa_cmp_w2=nrm((DEPTH, 2, NSA_CMP_HIDDEN, HEAD_DIM), NSA_CMP_HIDDEN ** -0.5),
        ssd_conv_w=nrm((DEPTH, SSD_CONV, SSD_CONV_DIM), SSD_CONV ** -0.5),
        ssd_conv_b=nrm((DEPTH, SSD_CONV_DIM), 0.02),
        ssd_dt_bias=ssd_dt_bias,
        ssd_a_log=ssd_a_log,
        ssd_d=gain((DEPTH, SSD_HEADS)),
        ssd_norm_g=gain((DEPTH, SSD_INNER)),
        conf_glu_b=nrm((DEPTH, 2 * CONF_CH), 0.02),
        conf_conv_w=nrm((DEPTH, CONF_WIDTH, CONF_CH), CONF_WIDTH ** -0.5),
        conf_conv_b=nrm((DEPTH, CONF_CH), 0.02),
        conf_ln_g=gain((DEPTH, CONF_CH)),
        conf_ln_b=nrm((DEPTH, CONF_CH), 0.02),
        w_branch=nrm((DEPTH, N_BRANCH, BRANCH_WIDTH, D_MODEL), BRANCH_WIDTH ** -0.5),
        b_gate=nrm((DEPTH, N_BRANCH, D_MODEL), 0.1),
        w_out=nrm((DEPTH, D_MODEL, D_MODEL), D_MODEL ** -0.5),
        g_ffn=gain((DEPTH, D_MODEL)),
        ff_w_gate=nrm((N_DENSE, D_MODEL, D_FF), D_MODEL ** -0.5),
        ff_w_up=nrm((N_DENSE, D_MODEL, D_FF), D_MODEL ** -0.5),
        ff_w_down=nrm((N_DENSE, D_FF, D_MODEL), D_FF ** -0.5),
        moe_w_router=nrm((N_MOE, D_MODEL, N_EXPERTS), D_MODEL ** -0.5),
        moe_b_router=nrm((N_MOE, N_EXPERTS), 0.01),
        moe_w_gate=nrm((N_MOE, N_EXPERTS, D_MODEL, D_FF_EXPERT), D_MODEL ** -0.5),
        moe_w_up=nrm((N_MOE, N_EXPERTS, D_MODEL, D_FF_EXPERT), D_MODEL ** -0.5),
        moe_w_down=nrm((N_MOE, N_EXPERTS, D_FF_EXPERT, D_MODEL), D_FF_EXPERT ** -0.5),
    )


def reference(x_prompt, x_sample, cache_nsa_kv, cache_nsa_win, cache_sb_kv, state_ssm, state_ssd_conv, state_conf_conv,
              page_table, g_mix, w_in, nsa_q_gain, nsa_k_gain, nsa_cmp_pe, nsa_cmp_w1, nsa_cmp_w2, ssd_conv_w, ssd_conv_b,
              ssd_dt_bias, ssd_a_log, ssd_d, ssd_norm_g, conf_glu_b, conf_conv_w, conf_conv_b, conf_ln_g, conf_ln_b,
              w_branch, b_gate, w_out, g_ffn, ff_w_gate, ff_w_up, ff_w_down, moe_w_router, moe_b_router, moe_w_gate,
              moe_w_up, moe_w_down):
    W = dict(g_mix=g_mix, w_in=w_in, nsa_q_gain=nsa_q_gain, nsa_k_gain=nsa_k_gain, nsa_cmp_pe=nsa_cmp_pe,
             nsa_cmp_w1=nsa_cmp_w1, nsa_cmp_w2=nsa_cmp_w2, ssd_conv_w=ssd_conv_w, ssd_conv_b=ssd_conv_b,
             ssd_dt_bias=ssd_dt_bias, ssd_a_log=ssd_a_log, ssd_d=ssd_d, ssd_norm_g=ssd_norm_g, conf_glu_b=conf_glu_b,
             conf_conv_w=conf_conv_w, conf_conv_b=conf_conv_b, conf_ln_g=conf_ln_g, conf_ln_b=conf_ln_b,
             w_branch=w_branch, b_gate=b_gate, w_out=w_out, g_ffn=g_ffn, ff_w_gate=ff_w_gate, ff_w_up=ff_w_up,
             ff_w_down=ff_w_down, moe_w_router=moe_w_router, moe_b_router=moe_b_router, moe_w_gate=moe_w_gate,
             moe_w_up=moe_w_up, moe_w_down=moe_w_down)
    past_len = page_table.shape[1] * PAGE_SIZE
    pos_p = jnp.arange(x_prompt.shape[1])
    pos_s = past_len + jnp.arange(x_sample.shape[1])
    y_prompt, y_sample = x_prompt, x_sample
    st_p, st_s = [], []
    for l in range(DEPTH):
        y_prompt, sp = trunk_layer(y_prompt, pos_p, l, W, None)
        past = dict(
            nsa_kv=[gather_pages(cache_nsa_kv, l, kind, page_table) for kind in range(4)],
            nsa_win=cache_nsa_win[l],
            sb_kv=[gather_pages(cache_sb_kv, l, kind, page_table) for kind in range(2)],
            ssm=state_ssm[l], ssd_conv=state_ssd_conv[l], conf_conv=state_conf_conv[l])
        y_sample, ss = trunk_layer(y_sample, pos_s, l, W, past)
        st_p.append(sp)
        st_s.append(ss)
    nsa_kv_p, nsa_win_p, sb_kv_p, ssm_p, ssd_conv_p, conf_conv_p = [jnp.stack(s, axis=0) for s in zip(*st_p)]
    nsa_kv_s, nsa_win_s, sb_kv_s, ssm_s, ssd_conv_s, conf_conv_s = [jnp.stack(s, axis=0) for s in zip(*st_s)]
    return (y_prompt, y_sample, nsa_kv_p, nsa_kv_s, nsa_win_p, nsa_win_s, sb_kv_p, sb_kv_s,
            ssm_p, ssm_s, ssd_conv_p, ssd_conv_s, conf_conv_p, conf_conv_s)
```

```python
import functools
import math

import jax
import jax.numpy as jnp
from jax import lax
from jax.experimental import pallas as pl
from jax.experimental.pallas import tpu as pltpu

f32 = jnp.float32
bf16 = jnp.bfloat16
i32 = jnp.int32

HEAD_DIM = 64
ROPE_DIM = 16
ROPE_THETA = 500000.0
NORM_EPS = 1e-6
LN_EPS = 1e-5
PAGE = 128
NSA_HEADS = 8
NSA_KV_HEADS = 2
NSA_GROUP = 4
CMP_LEN = 32
CMP_STRIDE = 16
CMP_HIDDEN = 128
SEL_LEN = 64
TOP_N = 16
WINDOW = 512
SSD_HEADS = 8
SSD_P = 64
SSD_N = 128
SSD_GROUPS = 2
SSD_CONV = 4
SSD_CHUNK = 128
SB_HEADS = 8
CONF_W = 31
N_EXPERTS = 8
TOP_K = 2

V7X_VMEM_BYTES = 64 * 1024 * 1024
VMEM_LIMIT = V7X_VMEM_BYTES - 8 * 1024 * 1024
LANES = 128
BIG = 1e30

MISC_GATE0 = 0
MISC_DT0 = 24


def _cparams(*sem):
    return pltpu.CompilerParams(dimension_semantics=sem, vmem_limit_bytes=VMEM_LIMIT)


def _dot(a, b):
    return jnp.dot(a, b, preferred_element_type=f32)


def _dot_nt(a, b):
    return lax.dot_general(a, b, (((1,), (1,)), ((), ())), preferred_element_type=f32)


def _dot_tn(a, b):
    return lax.dot_general(a, b, (((0,), (0,)), ((), ())), preferred_element_type=f32)


def _dot_hi(a, b):
    return jnp.dot(a, b, preferred_element_type=f32, precision=lax.Precision.HIGHEST)


def _split_dot(x, u):
    hi = x.astype(bf16)
    lo = (x - hi.astype(f32)).astype(bf16)
    return _dot(hi, u) + _dot(lo, u)


def _sigmoid(x):
    return 1.0 / (1.0 + jnp.exp(-x))


def _silu(x):
    return x * _sigmoid(x)


def _softplus(x):
    return jnp.maximum(x, 0.0) + jnp.log1p(jnp.exp(-jnp.abs(x)))


def _iota(shape, dim):
    return lax.broadcasted_iota(i32, shape, dim)


def _inproj_kernel(x_ref, g_ref, w_ref, *o_refs, widths):
    x = x_ref[...]
    h = (x * lax.rsqrt(jnp.mean(x * x, axis=-1, keepdims=True) + NORM_EPS) * g_ref[...]).astype(bf16)
    off = 0
    for o_ref, wd in zip(o_refs, widths):
        o_ref[...] = _dot(h, w_ref[:, off:off + wd]).astype(o_ref.dtype)
        off += wd


def _inproj(x, g, w, widths, dtypes, tm):
    m, d = x.shape
    return pl.pallas_call(
        functools.partial(_inproj_kernel, widths=widths),
        grid=(m // tm,),
        in_specs=[pl.BlockSpec((tm, d), lambda i: (i, 0)),
                  pl.BlockSpec((1, d), lambda i: (0, 0)),
                  pl.BlockSpec(w.shape, lambda i: (0, 0))],
        out_specs=[pl.BlockSpec((tm, wd), lambda i: (i, 0)) for wd in widths],
        out_shape=[jax.ShapeDtypeStruct((m, wd), dt) for wd, dt in zip(widths, dtypes)],
        compiler_params=_cparams("parallel"),
        name="inproj",
    )(x, g, w)


def _nsa_prep_kernel(q_ref, kv_ref, cos_ref, s1_ref, s2_ref, qg_ref, kg_ref,
                     qn_ref, rows_ref, win_ref, cmpk_ref, cmpv_ref, selkv_ref, winkv_ref):
    cs, s1, s2 = cos_ref[...], s1_ref[...], s2_ref[...]
    lo = _iota((1, LANES), 1) < HEAD_DIM

    def normrope(x, gain):
        sq = x * x
        m0 = jnp.sum(jnp.where(lo, sq, 0.0), axis=-1, keepdims=True) * (1.0 / HEAD_DIM)
        m1 = jnp.sum(jnp.where(lo, 0.0, sq), axis=-1, keepdims=True) * (1.0 / HEAD_DIM)
        y = x * jnp.where(lo, lax.rsqrt(m0 + NORM_EPS), lax.rsqrt(m1 + NORM_EPS)) * gain
        return y * cs + pltpu.roll(y, LANES - ROPE_DIM // 2, 1) * s1 + pltpu.roll(y, ROPE_DIM // 2, 1) * s2

    qg = qg_ref[...]
    for j in range(NSA_HEADS // 2):
        sl = slice(j * LANES, (j + 1) * LANES)
        qn_ref[:, sl] = (normrope(q_ref[:, sl], qg) * (HEAD_DIM ** -0.5)).astype(bf16)
    kn = [normrope(kv_ref[:, (2 * j) * LANES:(2 * j + 1) * LANES], kg_ref[j:j + 1, :]) for j in range(3)]
    vv = [kv_ref[:, (2 * j + 1) * LANES:(2 * j + 2) * LANES] for j in range(3)]
    rows_ref[:, 0 * LANES:1 * LANES] = kn[0]
    rows_ref[:, 1 * LANES:2 * LANES] = vv[0]
    rows_ref[:, 2 * LANES:3 * LANES] = kn[1]
    rows_ref[:, 3 * LANES:4 * LANES] = vv[1]
    win_ref[:, 0:LANES] = kn[2]
    win_ref[:, LANES:2 * LANES] = vv[2]
    cmpk_ref[...] = kn[0].astype(bf16)
    cmpv_ref[...] = vv[0].astype(bf16)
    selkv_ref[:, 0:LANES] = kn[1].astype(bf16)
    selkv_ref[:, LANES:2 * LANES] = vv[1].astype(bf16)
    winkv_ref[:, 0:LANES] = kn[2].astype(bf16)
    winkv_ref[:, LANES:2 * LANES] = vv[2].astype(bf16)


def _rope_tables(pos):
    half = ROPE_DIM // 2
    inv = ROPE_THETA ** (-jnp.arange(half, dtype=f32) / half)
    ang = pos.astype(f32)[:, None] * inv[None, :]
    cos, sin = jnp.cos(ang), jnp.sin(ang)
    n = pos.shape[0]
    z = lambda k: jnp.zeros((n, k), f32)
    c64 = jnp.concatenate([cos, cos, jnp.ones((n, HEAD_DIM - ROPE_DIM), f32)], axis=1)
    s1 = jnp.concatenate([-sin, z(HEAD_DIM - half)], axis=1)
    s2 = jnp.concatenate([z(half), sin, z(HEAD_DIM - ROPE_DIM)], axis=1)
    return tuple(jnp.tile(t, (1, 2)) for t in (c64, s1, s2))


def _nsa_prep(q, kv, tables, q_gain, k_gain, nb, tm):
    m = q.shape[0]
    nt = m // nb // tm
    row = lambda b, i: (b * nt + i, 0)
    tab = lambda b, i: (i, 0)
    cst = lambda b, i: (0, 0)
    qg = jnp.tile(q_gain.reshape(1, HEAD_DIM), (1, 2))
    kg = jnp.tile(k_gain.reshape(3, HEAD_DIM), (1, 2))
    kg = jnp.concatenate([kg, jnp.zeros((5, LANES), f32)], axis=0)
    widths = (512, 512, 256, 128, 128, 256, 256)
    dts = (bf16, f32, f32, bf16, bf16, bf16, bf16)
    return pl.pallas_call(
        _nsa_prep_kernel,
        grid=(nb, nt),
        in_specs=[pl.BlockSpec((tm, 512), row), pl.BlockSpec((tm, 768), row),
                  pl.BlockSpec((tm, LANES), tab), pl.BlockSpec((tm, LANES), tab), pl.BlockSpec((tm, LANES), tab),
                  pl.BlockSpec((1, LANES), cst), pl.BlockSpec((8, LANES), cst)],
        out_specs=[pl.BlockSpec((tm, w), row) for w in widths],
        out_shape=[jax.ShapeDtypeStruct((m, w), d) for w, d in zip(widths, dts)],
        compiler_params=_cparams("parallel", "parallel"),
        name="nsa_prep",
    )(q, kv, *tables, qg, kg)


def _nsa_cmp_kernel(ck_ref, cv_ref, nk_ref, nv_ref, w1a_ref, w1b_ref, c_ref, w2_ref, o_ref, sh_ref, *, tc):
    for kind, (x_ref, n_ref) in enumerate(((ck_ref, nk_ref), (cv_ref, nv_ref))):
        x = x_ref[...]
        pa = _dot(x, w1a_ref[kind])
        sh_ref[pl.ds(0, tc), :] = _dot(x, w1b_ref[kind])
        sh_ref[pl.ds(tc, 8), :] = _dot(n_ref[...], w1b_ref[kind])
        hid = pa + sh_ref[pl.ds(1, tc), :] + c_ref[kind]
        o_ref[:, kind * LANES:(kind + 1) * LANES] = _dot(jax.nn.gelu(hid).astype(bf16), w2_ref[kind])


def _nsa_compress(ck, cv, w1a, w1b, cvec, w2bd, tc):
    r = ck.shape[0]
    nxt = lambda i: (jnp.minimum((i + 1) * (tc // 8), r // 8 - 1), 0)
    full = lambda a: pl.BlockSpec(a.shape, lambda i: (0,) * a.ndim)
    return pl.pallas_call(
        functools.partial(_nsa_cmp_kernel, tc=tc),
        grid=(r // tc,),
        in_specs=[pl.BlockSpec((tc, 2048), lambda i: (i, 0)), pl.BlockSpec((tc, 2048), lambda i: (i, 0)),
                  pl.BlockSpec((8, 2048), nxt), pl.BlockSpec((8, 2048), nxt),
                  full(w1a), full(w1b), full(cvec), full(w2bd)],
        out_specs=pl.BlockSpec((tc, 256), lambda i: (i, 0)),
        out_shape=jax.ShapeDtypeStruct((r, 256), f32),
        scratch_shapes=[pltpu.VMEM((tc + 8, 256), f32)],
        compiler_params=_cparams("parallel"),
        name="nsa_compress",
    )(ck, cv, ck, cv, w1a, w1b, cvec, w2bd)


def _cmp_weights(pe, w1, w2):
    def two_head(w):
        w = w.reshape(CMP_STRIDE, 1, HEAD_DIM, 1, CMP_HIDDEN)
        eye = jnp.eye(2, dtype=f32).reshape(1, 2, 1, 2, 1)
        return (w * eye).reshape(CMP_STRIDE * 2 * HEAD_DIM, 2 * CMP_HIDDEN)
    half = CMP_STRIDE * HEAD_DIM
    w1a = jnp.stack([two_head(w1[k, :half]) for k in range(2)]).astype(bf16)
    w1b = jnp.stack([two_head(w1[k, half:]) for k in range(2)]).astype(bf16)
    c = jnp.stack([jnp.tile(jnp.dot(pe[k].reshape(1, -1), w1[k], precision=lax.Precision.HIGHEST), (1, 2))
                   for k in range(2)])
    eye2 = jnp.eye(2, dtype=f32)
    w2bd = jnp.stack([jnp.kron(eye2, w2[k]) for k in range(2)]).astype(bf16)
    return w1a, w1b, c, w2bd


def _masked_softmax(s, mask):
    s = jnp.where(mask, s, -BIG)
    m = jnp.max(s, axis=-1, keepdims=True)
    e = jnp.where(mask, jnp.exp(s - m), 0.0)
    den = jnp.sum(e, axis=-1, keepdims=True)
    return e / jnp.where(den > 0, den, 1.0)


def _nsa_attn_kernel(q_ref, misc_ref, kvc_ref, sel_ref, win_ref, cover_ref, o_ref, *,
                     tq, tk, qbase, nch, nselp, win_base, win_len, win_slide):
    i = pl.program_id(1)
    q0 = qbase + i * tq
    g4 = NSA_GROUP
    qpos = q0 + _iota((tq, 1), 0)
    qpos4 = jnp.concatenate([qpos] * g4, axis=0)
    gates = _sigmoid(misc_ref[...])
    cend = _iota((1, nch), 1) * CMP_STRIDE + (CMP_LEN - 1)
    blk = _iota((1, nselp), 1)
    blkf = blk.astype(f32)
    cur = jnp.right_shift(qpos, 6)
    forced = (blk == 0) | (blk == cur) | (blk == cur - 1)
    valid = blk * SEL_LEN <= qpos
    cover = cover_ref[...]
    n_t = (q0 + tq - 1) // tk + 1
    if win_slide:
        w0 = pl.multiple_of(jnp.maximum(q0 - WINDOW, 0), 128)
    else:
        w0 = 0
    kwpos = win_base + w0 + _iota((1, win_len), 1)
    dpos = qpos4 - kwpos
    m_w = (dpos >= 0) & (dpos < WINDOW)

    for kvh in range(NSA_KV_HEADS):
        hs = slice(kvh * HEAD_DIM, (kvh + 1) * HEAD_DIM)
        vs = slice(LANES + kvh * HEAD_DIM, LANES + (kvh + 1) * HEAD_DIM)
        qg = jnp.concatenate([q_ref[:, (kvh * g4 + g) * HEAD_DIM:(kvh * g4 + g + 1) * HEAD_DIM]
                              for g in range(g4)], axis=0)
        kc = kvc_ref[0, :, hs].astype(bf16)
        vc = kvc_ref[0, :, vs].astype(bf16)
        p_c = _masked_softmax(_dot_nt(qg, kc), cend <= qpos4)
        o_c = _dot(p_c.astype(bf16), vc)
        psum = p_c[0:tq]
        for g in range(1, g4):
            psum = psum + p_c[g * tq:(g + 1) * tq]
        imp = _split_dot(psum, cover)
        v0 = jnp.where(forced, BIG, jnp.where(valid, imp, -BIG))

        def pick(_, carry):
            v, selm = carry
            m = jnp.max(v, axis=-1, keepdims=True)
            first = jnp.min(jnp.where(v == m, blkf, float(nselp)), axis=-1, keepdims=True)
            hit = blkf == first
            return jnp.where(hit, -3.0e38, v), jnp.where(hit, 1.0, selm)

        _, selm = lax.fori_loop(0, TOP_N, pick, (v0, jnp.zeros((tq, nselp), f32)))
        selb = selm.astype(bf16)

        def tile(t, carry):
            m_i, l_i, acc = carry
            k0 = pl.multiple_of(t * tk, tk)
            kt = sel_ref[0, pl.ds(k0, tk), hs]
            vt = sel_ref[0, pl.ds(k0, tk), vs]
            s = _dot_nt(qg, kt)
            kpos = k0 + _iota((1, tk), 1)
            expand = jnp.where(_iota((nselp, tk), 0) == jnp.right_shift(k0 + _iota((nselp, tk), 1), 6), 1.0, 0.0).astype(bf16)
            okf = jnp.where(kpos <= qpos, _dot(selb, expand), 0.0)
            ok = jnp.concatenate([okf] * g4, axis=0) > 0.5
            s = jnp.where(ok, s, -BIG)
            m_new = jnp.maximum(m_i, jnp.max(s, axis=-1, keepdims=True))
            alpha = jnp.exp(m_i - m_new)
            p = jnp.where(ok, jnp.exp(s - m_new), 0.0)
            l_new = alpha * l_i + jnp.sum(p, axis=-1, keepdims=True)
            acc = alpha * acc + _dot(p.astype(bf16), vt)
            return m_new, l_new, acc

        init = (jnp.full((g4 * tq, 1), -BIG, f32), jnp.zeros((g4 * tq, 1), f32), jnp.zeros((g4 * tq, HEAD_DIM), f32))
        _, l_s, acc_s = lax.fori_loop(0, n_t, tile, init)
        o_s = acc_s / jnp.where(l_s > 0, l_s, 1.0)
        kw = win_ref[0, pl.ds(w0, win_len), hs]
        vw = win_ref[0, pl.ds(w0, win_len), vs]
        p_w = _masked_softmax(_dot_nt(qg, kw), m_w)
        o_w = _dot(p_w.astype(bf16), vw)
        for g in range(g4):
            h = kvh * g4 + g
            rs = slice(g * tq, (g + 1) * tq)
            o = (gates[:, 3 * h:3 * h + 1] * o_c[rs] + gates[:, 3 * h + 1:3 * h + 2] * o_s[rs]
                 + gates[:, 3 * h + 2:3 * h + 3] * o_w[rs])
            o_ref[:, h * HEAD_DIM:(h + 1) * HEAD_DIM] = o.astype(o_ref.dtype)


def _nsa_attn(qn, misc, kvc, selkv, winkv, ns, tq, tk, qbase, win_base, win_len, win_slide):
    m = qn.shape[0]
    nq = m // ns // tq
    nch = kvc.shape[1]
    t_keys = selkv.shape[1]
    nselp = -(-(t_keys // SEL_LEN) // LANES) * LANES
    ci = jnp.arange(nch)[:, None] * CMP_STRIDE
    cj = jnp.arange(nselp)[None, :] * SEL_LEN
    cover = ((ci < cj + SEL_LEN) & (ci + CMP_LEN > cj)).astype(bf16)
    row = lambda s, i: (s * nq + i, 0)
    seq = lambda s, i: (s, 0, 0)
    return pl.pallas_call(
        functools.partial(_nsa_attn_kernel, tq=tq, tk=tk, qbase=qbase, nch=nch, nselp=nselp,
                          win_base=win_base, win_len=win_len, win_slide=win_slide),
        grid=(ns, nq),
        in_specs=[pl.BlockSpec((tq, 512), row), pl.BlockSpec((tq, LANES), row),
                  pl.BlockSpec((1, nch, 256), seq), pl.BlockSpec((1, t_keys, 256), seq),
                  pl.BlockSpec((1, winkv.shape[1], 256), seq),
                  pl.BlockSpec((nch, nselp), lambda s, i: (0, 0))],
        out_specs=pl.BlockSpec((tq, 512), row),
        out_shape=jax.ShapeDtypeStruct((m, 512), bf16),
        compiler_params=_cparams("parallel", "arbitrary"),
        name="nsa_attn",
    )(qn, misc, kvc, selkv, winkv, cover)


def _page_copy_kernel(tbl_ref, *refs, pg):
    del tbl_ref
    pages, new_ref = refs[:pg], refs[pg]
    cmpk_ref, cmpv_ref, selkv_ref = refs[pg + 1:]
    g = pl.program_id(1)
    last = pl.num_programs(1) - 1

    @pl.when(g < last)
    def _():
        for j in range(pg):
            rs = pl.ds(j * PAGE, PAGE)
            cmpk_ref[0, rs, :] = pages[j][0, :, 0:LANES].astype(bf16)
            cmpv_ref[0, rs, :] = pages[j][0, :, LANES:2 * LANES].astype(bf16)
            selkv_ref[0, rs, :] = pages[j][0, :, 2 * LANES:4 * LANES].astype(bf16)

    @pl.when(g == last)
    def _():
        cmpk_ref[0] = jnp.zeros((pg * PAGE, LANES), bf16)
        cmpv_ref[0] = jnp.zeros((pg * PAGE, LANES), bf16)
        selkv_ref[0] = jnp.zeros((pg * PAGE, 2 * LANES), bf16)
        cmpk_ref[0, pl.ds(0, PAGE), :] = new_ref[0, :, 0:LANES].astype(bf16)
        cmpv_ref[0, pl.ds(0, PAGE), :] = new_ref[0, :, LANES:2 * LANES].astype(bf16)
        selkv_ref[0, pl.ds(0, PAGE), :] = new_ref[0, :, 2 * LANES:4 * LANES].astype(bf16)


def _page_copy(pool, tbl, new_rows, layer_off, pg):
    s, n_pages = tbl.shape
    ng = n_pages // pg
    t_pad = (ng + 1) * pg * PAGE

    def pspec(j):
        return pl.BlockSpec((1, PAGE, 512), lambda b, g, t: (layer_off + t[b, jnp.minimum(g, ng - 1) * pg + j], 0, 0))
    out = lambda w: pl.BlockSpec((1, pg * PAGE, w), lambda b, g, t: (b, g, 0))
    gs = pltpu.PrefetchScalarGridSpec(
        num_scalar_prefetch=1, grid=(s, ng + 1),
        in_specs=[pspec(j) for j in range(pg)] + [pl.BlockSpec((1, PAGE, 512), lambda b, g, t: (b, 0, 0))],
        out_specs=[out(LANES), out(LANES), out(2 * LANES)])
    return pl.pallas_call(
        functools.partial(_page_copy_kernel, pg=pg),
        grid_spec=gs,
        out_shape=[jax.ShapeDtypeStruct((s, t_pad, LANES), bf16), jax.ShapeDtypeStruct((s, t_pad, LANES), bf16),
                   jax.ShapeDtypeStruct((s, t_pad, 2 * LANES), bf16)],
        compiler_params=_cparams("parallel", "arbitrary"),
        name="nsa_page_copy",
    )(tbl, *([pool] * pg), new_rows)


def _ssd_kernel(xbc_ref, z_ref, misc_ref, prev_ref, h0_ref, cw_ref, cb_ref, dtb_ref, alog_ref, dsk_ref, ng_ref,
                y_ref, hn_ref, cs_ref, xs_ref, hst_ref, *, lc, t_valid):
    c = pl.program_id(1)
    inner = SSD_HEADS * SSD_P

    @pl.when(c == 0)
    def _():
        xs_ref[pl.ds(0, 8), :] = prev_ref[0]
        hst_ref[...] = h0_ref[0]

    xs_ref[pl.ds(8, lc), :] = xbc_ref[...]
    conv = cb_ref[...]
    for k in range(SSD_CONV):
        conv = conv + cw_ref[k:k + 1, :] * xs_ref[pl.ds(8 - (SSD_CONV - 1) + k, lc), :]
    act = _silu(conv)
    cs_ref[0] = xs_ref[pl.ds(t_valid, 8), :]
    xs_ref[pl.ds(0, 8), :] = xs_ref[pl.ds(lc, 8), :]

    rows = _iota((lc, 1), 0)
    lane = _iota((1, LANES), 1)
    dt_lane = (lane >= MISC_DT0) & (lane < MISC_DT0 + SSD_HEADS)
    dt = _softplus(misc_ref[...] + dtb_ref[...])
    dt = jnp.where(dt_lane & (rows < t_valid), dt, 0.0)
    dta = dt * jnp.where(dt_lane, -jnp.exp(alog_ref[...]), 0.0)
    tt = _iota((lc, lc), 0)
    ss = _iota((lc, lc), 1)
    tril = tt >= ss
    la = _dot_hi(jnp.where(tril, 1.0, 0.0), dta)
    z = z_ref[...]
    dsk = dsk_ref[...]
    ys = []
    for g in range(SSD_GROUPS):
        bg = act[:, inner + g * SSD_N: inner + (g + 1) * SSD_N]
        cg = act[:, inner + (SSD_GROUPS + g) * SSD_N: inner + (SSD_GROUPS + g + 1) * SSD_N]
        bgb = bg.astype(bf16)
        cb = _dot_nt(cg.astype(bf16), bgb)
        for hh in range(SSD_HEADS // SSD_GROUPS):
            h = g * (SSD_HEADS // SSD_GROUPS) + hh
            ln = MISC_DT0 + h
            col = la[:, ln:ln + 1]
            row = jnp.sum(jnp.where(tt <= ss, dta[:, ln:ln + 1], 0.0), axis=0, keepdims=True)
            decay = jnp.exp(jnp.where(tril, col - row, -jnp.inf))
            xh = act[:, h * SSD_P:(h + 1) * SSD_P]
            xdt = xh * dt[:, ln:ln + 1]
            y = _dot((cb * decay).astype(bf16), xdt.astype(bf16))
            hs = hst_ref[h * SSD_P:(h + 1) * SSD_P, :]
            y = y + _dot_nt((cg * jnp.exp(col)).astype(bf16), hs.astype(bf16))
            last = la[lc - 1:lc, ln:ln + 1]
            sck = _dot_tn((xdt * jnp.exp(last - col)).astype(bf16), bgb)
            hst_ref[h * SSD_P:(h + 1) * SSD_P, :] = hs * jnp.exp(last) + sck
            y = y + dsk[:, h * SSD_P:(h + 1) * SSD_P] * xh
            ys.append(y * _silu(z[:, h * SSD_P:(h + 1) * SSD_P]))
    gw = inner // SSD_GROUPS
    hpg = SSD_HEADS // SSD_GROUPS
    for g in range(SSD_GROUPS):
        yg = jnp.concatenate(ys[g * hpg:(g + 1) * hpg], axis=1)
        yn = yg * lax.rsqrt(jnp.mean(yg * yg, axis=-1, keepdims=True) + NORM_EPS) * ng_ref[:, g * gw:(g + 1) * gw]
        y_ref[:, g * gw:(g + 1) * gw] = yn.astype(y_ref.dtype)

    @pl.when(c == pl.num_programs(1) - 1)
    def _():
        hn_ref[0] = hst_ref[...]


def _ssd(xbc, z, misc, prev8, h0, conv_w, conv_b, dt_bias, a_log, d_skip, norm_g, ns, lc, t_valid):
    m = xbc.shape[0]
    nc = m // ns // lc
    inner = SSD_HEADS * SSD_P
    row = lambda s, c: (s * nc + c, 0)
    seq = lambda s, c: (s, 0, 0)
    cst = lambda s, c: (0, 0)
    cw = jnp.concatenate([conv_w, jnp.zeros((8 - SSD_CONV, conv_w.shape[1]), f32)], axis=0)
    lanepad = lambda v: jnp.zeros((1, LANES), f32).at[0, MISC_DT0:MISC_DT0 + SSD_HEADS].set(v)
    return pl.pallas_call(
        functools.partial(_ssd_kernel, lc=lc, t_valid=t_valid),
        grid=(ns, nc),
        in_specs=[pl.BlockSpec((lc, 1024), row), pl.BlockSpec((lc, inner), row), pl.BlockSpec((lc, LANES), row),
                  pl.BlockSpec((1, 8, 1024), seq), pl.BlockSpec((1, inner, SSD_N), seq),
                  pl.BlockSpec((8, 1024), cst), pl.BlockSpec((1, 1024), cst),
                  pl.BlockSpec((1, LANES), cst), pl.BlockSpec((1, LANES), cst),
                  pl.BlockSpec((1, inner), cst), pl.BlockSpec((1, inner), cst)],
        out_specs=[pl.BlockSpec((lc, inner), row), pl.BlockSpec((1, inner, SSD_N), seq), pl.BlockSpec((1, 8, 1024), seq)],
        out_shape=[jax.ShapeDtypeStruct((m, inner), bf16), jax.ShapeDtypeStruct((ns, inner, SSD_N), f32),
                   jax.ShapeDtypeStruct((ns, 8, 1024), f32)],
        scratch_shapes=[pltpu.VMEM((lc + 8, 1024), f32), pltpu.VMEM((inner, SSD_N), f32)],
        compiler_params=_cparams("parallel", "arbitrary"),
        name="ssd",
    )(xbc, z, misc, prev8, h0, cw, conv_b.reshape(1, -1), lanepad(dt_bias), lanepad(a_log),
      jnp.repeat(d_skip, SSD_P).reshape(1, -1), norm_g.reshape(1, -1))


def _sb_tile(z, v, carry, u, mask):
    lk = -_softplus(z)
    lkm = lk if mask is None else jnp.where(mask, lk, 0.0)
    after = _split_dot(lkm, u) + carry
    w = jnp.exp(z + lk + after)
    if mask is not None:
        w = jnp.where(mask, w, 0.0)
    return _dot(w.astype(bf16), v), carry + jnp.sum(lkm, axis=-1, keepdims=True)


def _strict_upper(n):
    return jnp.where(_iota((n, n), 0) > _iota((n, n), 1), 1.0, 0.0).astype(bf16)


def _sb_prefill_kernel(q_ref, k_ref, v_ref, o_ref, *, tq):
    i = pl.program_id(2)
    u = _strict_upper(tq)
    diag_mask = _iota((tq, tq), 1) < _iota((tq, tq), 0)
    for hh in range(2):
        hs = slice(hh * HEAD_DIM, (hh + 1) * HEAD_DIM)
        q = q_ref[:, hs]
        d0 = pl.multiple_of(i * tq, tq)
        acc, carry = _sb_tile(_dot_nt(q, k_ref[0, pl.ds(d0, tq), hs]), v_ref[0, pl.ds(d0, tq), hs],
                              jnp.zeros((tq, 1), f32), u, diag_mask)

        def body(j, c):
            acc, carry = c
            k0 = pl.multiple_of((i - 1 - j) * tq, tq)
            a, carry = _sb_tile(_dot_nt(q, k_ref[0, pl.ds(k0, tq), hs]), v_ref[0, pl.ds(k0, tq), hs], carry, u, None)
            return acc + a, carry

        acc, _ = lax.fori_loop(0, i, body, (acc, carry))
        o_ref[:, hs] = acc.astype(o_ref.dtype)


def _sb_prefill(q, kvb, nb, tq):
    m = q.shape[0]
    t = m // nb
    nq = t // tq
    return pl.pallas_call(
        functools.partial(_sb_prefill_kernel, tq=tq),
        grid=(nb, SB_HEADS // 2, nq),
        in_specs=[pl.BlockSpec((tq, LANES), lambda b, h, i: (b * nq + i, h)),
                  pl.BlockSpec((1, t, LANES), lambda b, h, i: (b, 0, h)),
                  pl.BlockSpec((1, t, LANES), lambda b, h, i: (b, 0, SB_HEADS // 2 + h))],
        out_specs=pl.BlockSpec((tq, LANES), lambda b, h, i: (b * nq + i, h)),
        out_shape=jax.ShapeDtypeStruct((m, 512), bf16),
        compiler_params=_cparams("parallel", "parallel", "arbitrary"),
        name="sb_prefill",
    )(q, kvb, kvb)


def _sb_decode_kernel(tbl_ref, *refs, pg, t_new, past_len):
    del tbl_ref
    kp, vp = refs[:pg], refs[pg:2 * pg]
    qbd_ref, kn_ref, vn_ref, o_ref, acc_ref, car_ref = refs[2 * pg:]
    g = pl.program_id(1)
    qbd = qbd_ref[0]
    u = _strict_upper(PAGE)

    def tile(k, v, mask):
        z = _dot_nt(qbd, k)
        a, car = _sb_tile(z, v, car_ref[...], u, mask)
        acc_ref[...] += a
        car_ref[...] = car

    @pl.when(g == 0)
    def _():
        acc_ref[...] = jnp.zeros_like(acc_ref)
        car_ref[...] = jnp.zeros_like(car_ref)
        qi = _iota((SB_HEADS * 8, PAGE), 0) % 8
        si = _iota((SB_HEADS * 8, PAGE), 1)
        tile(kn_ref[0], vn_ref[0], (si < qi) & (si < t_new))

    for j in range(pg):
        tile(kp[j][0].astype(bf16), vp[j][0].astype(bf16), None)

    @pl.when(g == pl.num_programs(1) - 1)
    def _():
        for h in range(SB_HEADS):
            o_ref[0, :, h * HEAD_DIM:(h + 1) * HEAD_DIM] = acc_ref[h * 8:(h + 1) * 8, h * HEAD_DIM:(h + 1) * HEAD_DIM].astype(o_ref.dtype)


def _sb_decode(qbd, knew, vnew, pool, tbl, layer_off, pg, t_new):
    s, n_pages = tbl.shape
    ng = n_pages // pg

    def pspec(j, kind):
        return pl.BlockSpec((1, PAGE, 512), lambda b, g, t: (layer_off + t[b, n_pages - 1 - (g * pg + j)], 0, kind))
    seq = lambda b, g, t: (b, 0, 0)
    gs = pltpu.PrefetchScalarGridSpec(
        num_scalar_prefetch=1, grid=(s, ng),
        in_specs=[pspec(j, 0) for j in range(pg)] + [pspec(j, 1) for j in range(pg)]
        + [pl.BlockSpec((1, SB_HEADS * 8, 512), seq), pl.BlockSpec((1, PAGE, 512), seq), pl.BlockSpec((1, PAGE, 512), seq)],
        out_specs=pl.BlockSpec((1, 8, 512), seq),
        scratch_shapes=[pltpu.VMEM((SB_HEADS * 8, 512), f32), pltpu.VMEM((SB_HEADS * 8, 1), f32)])
    return pl.pallas_call(
        functools.partial(_sb_decode_kernel, pg=pg, t_new=t_new, past_len=n_pages * PAGE),
        grid_spec=gs,
        out_shape=jax.ShapeDtypeStruct((s, 8, 512), bf16),
        compiler_params=_cparams("parallel", "arbitrary"),
        name="sb_decode",
    )(tbl, *([pool] * (2 * pg)), qbd, knew, vnew)


def _conf_kernel(x_ref, prev_ref, gb_ref, cw_ref, cb_ref, lg_ref, lb_ref, o_ref, cs_ref, us_ref, *, lc, t_valid):
    c = pl.program_id(1)
    ch = o_ref.shape[-1]

    @pl.when(c == 0)
    def _():
        us_ref[pl.ds(0, 32), :] = prev_ref[0]

    v = x_ref[...] + gb_ref[...]
    us_ref[pl.ds(32, lc), :] = v[:, :ch] * _sigmoid(v[:, ch:])
    conv = cb_ref[...]
    for k in range(CONF_W):
        conv = conv + cw_ref[k:k + 1, :] * us_ref[pl.ds(32 - (CONF_W - 1) + k, lc), :]
    mu = jnp.mean(conv, axis=-1, keepdims=True)
    d = conv - mu
    var = jnp.mean(d * d, axis=-1, keepdims=True)
    o_ref[...] = _silu(d * lax.rsqrt(var + LN_EPS) * lg_ref[...] + lb_ref[...]).astype(o_ref.dtype)
    cs_ref[0] = us_ref[pl.ds(t_valid, 32), :]
    us_ref[pl.ds(0, 32), :] = us_ref[pl.ds(lc, 32), :]


def _conformer(x, prev32, glu_b, conv_w, conv_b, ln_g, ln_b, ns, lc, t_valid):
    m = x.shape[0]
    ch = x.shape[1] // 2
    nc = m // ns // lc
    row = lambda s, c: (s * nc + c, 0)
    seq = lambda s, c: (s, 0, 0)
    cst = lambda s, c: (0, 0)
    cw = jnp.concatenate([conv_w, jnp.zeros((32 - CONF_W, ch), f32)], axis=0)
    return pl.pallas_call(
        functools.partial(_conf_kernel, lc=lc, t_valid=t_valid),
        grid=(ns, nc),
        in_specs=[pl.BlockSpec((lc, 2 * ch), row), pl.BlockSpec((1, 32, ch), seq), pl.BlockSpec((1, 2 * ch), cst),
                  pl.BlockSpec((32, ch), cst), pl.BlockSpec((1, ch), cst), pl.BlockSpec((1, ch), cst),
                  pl.BlockSpec((1, ch), cst)],
        out_specs=[pl.BlockSpec((lc, ch), row), pl.BlockSpec((1, 32, ch), seq)],
        out_shape=[jax.ShapeDtypeStruct((m, ch), bf16), jax.ShapeDtypeStruct((ns, 32, ch), f32)],
        scratch_shapes=[pltpu.VMEM((lc + 32, ch), f32)],
        compiler_params=_cparams("parallel", "arbitrary"),
        name="conformer",
    )(x, prev32, glu_b.reshape(1, -1), cw, conv_b.reshape(1, -1), ln_g.reshape(1, -1), ln_b.reshape(1, -1))


def _merge_kernel(x_ref, g_ref, wg_ref, bg_ref, b0_ref, b1_ref, b2_ref, b3_ref, wb_ref, wo_ref, o_ref):
    x = x_ref[...]
    d = x.shape[1]
    h = (x * lax.rsqrt(jnp.mean(x * x, axis=-1, keepdims=True) + NORM_EPS) * g_ref[...]).astype(bf16)
    m = jnp.zeros(x.shape, f32)
    for i, b_ref in enumerate((b0_ref, b1_ref, b2_ref, b3_ref)):
        gate = _sigmoid(_dot(h, wg_ref[:, i * d:(i + 1) * d]) + bg_ref[:, i * d:(i + 1) * d])
        m = m + gate * _dot(b_ref[...], wb_ref[i])
    o_ref[...] = x + _dot(m.astype(bf16), wo_ref[...])


def _merge(x, g, wg, bg, branches, wb, wo, tm):
    m, d = x.shape
    row = lambda i: (i, 0)
    full = lambda a: pl.BlockSpec(a.shape, lambda i: (0,) * a.ndim)
    return pl.pallas_call(
        _merge_kernel,
        grid=(m // tm,),
        in_specs=[pl.BlockSpec((tm, d), row), full(g), full(wg), full(bg)]
        + [pl.BlockSpec((tm, b.shape[1]), row) for b in branches] + [full(wb), full(wo)],
        out_specs=pl.BlockSpec((tm, d), row),
        out_shape=jax.ShapeDtypeStruct((m, d), f32),
        compiler_params=_cparams("parallel"),
        name="merge",
    )(x, g, wg, bg, *branches, wb, wo)


def _ffn_kernel(x_ref, g_ref, wg_ref, wu_ref, wd_ref, o_ref, h_ref, acc_ref):
    f = pl.program_id(1)

    @pl.when(f == 0)
    def _():
        x = x_ref[...]
        h_ref[...] = (x * lax.rsqrt(jnp.mean(x * x, axis=-1, keepdims=True) + NORM_EPS) * g_ref[...]).astype(bf16)
        acc_ref[...] = jnp.zeros_like(acc_ref)

    h = h_ref[...]
    a = _silu(_dot(h, wg_ref[...])) * _dot(h, wu_ref[...])
    acc_ref[...] += _dot(a.astype(bf16), wd_ref[...])

    @pl.when(f == pl.num_programs(1) - 1)
    def _():
        o_ref[...] = x_ref[...] + acc_ref[...]


def _ffn(x, g, wg, wu, wd, tm, tf):
    m, d = x.shape
    nf = wg.shape[1] // tf
    return pl.pallas_call(
        _ffn_kernel,
        grid=(m // tm, nf),
        in_specs=[pl.BlockSpec((tm, d), lambda i, f: (i, 0)), pl.BlockSpec((1, d), lambda i, f: (0, 0)),
                  pl.BlockSpec((d, tf), lambda i, f: (0, f)), pl.BlockSpec((d, tf), lambda i, f: (0, f)),
                  pl.BlockSpec((tf, d), lambda i, f: (f, 0))],
        out_specs=pl.BlockSpec((tm, d), lambda i, f: (i, 0)),
        out_shape=jax.ShapeDtypeStruct((m, d), f32),
        scratch_shapes=[pltpu.VMEM((tm, d), bf16), pltpu.VMEM((tm, d), f32)],
        compiler_params=_cparams("parallel", "arbitrary"),
        name="ffn",
    )(x, g, wg, wu, wd)


def _route_kernel(x_ref, g_ref, wr_ref, br_ref, h_ref, r_ref):
    x = x_ref[...]
    hn = x * lax.rsqrt(jnp.mean(x * x, axis=-1, keepdims=True) + NORM_EPS) * g_ref[...]
    h_ref[...] = hn.astype(bf16)
    lane = _iota((1, LANES), 1)
    lanef = lane.astype(f32)
    logits = jnp.where(lane < N_EXPERTS, _dot_hi(hn, wr_ref[...]) + br_ref[...], -BIG)
    l1 = jnp.max(logits, axis=-1, keepdims=True)
    e1 = jnp.min(jnp.where(logits == l1, lanef, float(LANES)), axis=-1, keepdims=True)
    rest = jnp.where(lanef == e1, -BIG, logits)
    l2 = jnp.max(rest, axis=-1, keepdims=True)
    e2 = jnp.min(jnp.where(rest == l2, lanef, float(LANES)), axis=-1, keepdims=True)
    ex = jnp.exp(l2 - l1)
    w1 = 1.0 / (1.0 + ex)
    w2 = ex / (1.0 + ex)
    r_ref[...] = jnp.where(lane == 0, e1, jnp.where(lane == 1, e2, jnp.where(lane == 2, w1, jnp.where(lane == 3, w2, 0.0))))


def _route(x, g, w_router, b_router, tm):
    m, d = x.shape
    wr = jnp.zeros((d, LANES), f32).at[:, :N_EXPERTS].set(w_router)
    br = jnp.zeros((1, LANES), f32).at[0, :N_EXPERTS].set(b_router)
    return pl.pallas_call(
        _route_kernel,
        grid=(m // tm,),
        in_specs=[pl.BlockSpec((tm, d), lambda i: (i, 0)), pl.BlockSpec((1, d), lambda i: (0, 0)),
                  pl.BlockSpec((d, LANES), lambda i: (0, 0)), pl.BlockSpec((1, LANES), lambda i: (0, 0))],
        out_specs=[pl.BlockSpec((tm, d), lambda i: (i, 0)), pl.BlockSpec((tm, LANES), lambda i: (i, 0))],
        out_shape=[jax.ShapeDtypeStruct((m, d), bf16), jax.ShapeDtypeStruct((m, LANES), f32)],
        compiler_params=_cparams("parallel"),
        name="moe_route",
    )(x, g, wr, br)


def _expert_kernel(be_ref, x_ref, wg_ref, wu_ref, wd_ref, o_ref, acc_ref):
    del be_ref
    f = pl.program_id(1)

    @pl.when(f == 0)
    def _():
        acc_ref[...] = jnp.zeros_like(acc_ref)

    x = x_ref[...]
    a = _silu(_dot(x, wg_ref[0])) * _dot(x, wu_ref[0])
    acc_ref[...] += _dot(a.astype(bf16), wd_ref[0])

    @pl.when(f == pl.num_programs(1) - 1)
    def _():
        o_ref[...] = acc_ref[...]


def _experts(xb, blk_e, wg, wu, wd, bm, tf):
    r, d = xb.shape
    nf = wg.shape[2] // tf
    gs = pltpu.PrefetchScalarGridSpec(
        num_scalar_prefetch=1, grid=(r // bm, nf),
        in_specs=[pl.BlockSpec((bm, d), lambda b, f, e: (b, 0)),
                  pl.BlockSpec((1, d, tf), lambda b, f, e: (e[b], 0, f)),
                  pl.BlockSpec((1, d, tf), lambda b, f, e: (e[b], 0, f)),
                  pl.BlockSpec((1, tf, d), lambda b, f, e: (e[b], f, 0))],
        out_specs=pl.BlockSpec((bm, d), lambda b, f, e: (b, 0)),
        scratch_shapes=[pltpu.VMEM((bm, d), f32)])
    return pl.pallas_call(
        _expert_kernel, grid_spec=gs,
        out_shape=jax.ShapeDtypeStruct((r, d), f32),
        compiler_params=_cparams("parallel", "arbitrary"),
        name="moe_experts",
    )(blk_e, xb, wg, wu, wd)


def _moe(x, g, w_router, b_router, wg, wu, wd, tm, bm, tf):
    m, d = x.shape
    hb, r = _route(x, g, w_router, b_router, tm)
    top_e = r[:, 0:TOP_K].astype(i32)
    top_w = r[:, TOP_K:2 * TOP_K]
    n_slot = m * TOP_K
    slot_e = top_e.reshape(-1)
    order = jnp.argsort(slot_e)
    e_sorted = slot_e[order]
    counts = jnp.sum(slot_e[:, None] == jnp.arange(N_EXPERTS)[None, :], axis=0).astype(i32)
    padded = (counts + bm - 1) // bm * bm
    pad_end = jnp.cumsum(padded)
    dest_sorted = (pad_end - padded)[e_sorted] + jnp.arange(n_slot) - (jnp.cumsum(counts) - counts)[e_sorted]
    n_blk = -(-n_slot // bm) + N_EXPERTS
    buf_tok = jnp.zeros((n_blk * bm,), i32).at[dest_sorted].set((order // TOP_K).astype(i32))
    dest = jnp.zeros((n_slot,), i32).at[order].set(dest_sorted.astype(i32))
    blk_e = jnp.minimum(jnp.sum(pad_end[None, :] <= (jnp.arange(n_blk) * bm)[:, None], axis=1), N_EXPERTS - 1).astype(i32)
    yb = _experts(hb[buf_tok], blk_e, wg, wu, wd, bm, tf)
    ys = yb[dest].reshape(m, TOP_K, d) * top_w[:, :, None]
    return x + (ys[:, 0] + ys[:, 1])


def _tile(m, pref):
    for t in pref:
        if m % t == 0:
            return t
    return m


def _prep_weights(W, l):
    d = W["w_in"].shape[1]
    w_in = W["w_in"][l]
    offs = [0]
    for wd in (512, 768, 24, 512, 1024, 8, 1536, 1024, 4 * d):
        offs.append(offs[-1] + wd)
    seg = lambda i: w_in[:, offs[i]:offs[i + 1]]
    nsa_q, nsa_kv, nsa_g, ssd_z, ssd_xbc, ssd_dt, sb_qkv, conf_in, br_gate = (seg(i) for i in range(9))
    misc = jnp.concatenate([nsa_g, ssd_dt, jnp.zeros((d, LANES - 32), f32)], axis=1)
    w_tok = jnp.concatenate([conf_in, ssd_xbc, sb_qkv[:, :512] * (HEAD_DIM ** -0.5), sb_qkv[:, 512:], sb_qkv[:, 512:],
                             nsa_q, ssd_z, nsa_kv, misc], axis=1).astype(bf16)
    P = dict(w_tok=w_tok, wg=br_gate.astype(bf16), bg=W["b_gate"][l].reshape(1, -1),
             wb=W["w_branch"][l].astype(bf16), wo=W["w_out"][l].astype(bf16),
             g_mix=W["g_mix"][l].reshape(1, -1), g_ffn=W["g_ffn"][l].reshape(1, -1))
    P["cmp"] = _cmp_weights(W["nsa_cmp_pe"][l], W["nsa_cmp_w1"][l], W["nsa_cmp_w2"][l])
    return P


TOK_WIDTHS = (1024, 1024, 512, 1024, 1024, 512, 512, 768, 128)
TOK_DTYPES = (f32, f32, bf16, f32, bf16, f32, f32, f32, f32)


def _token_mixers(x2d, l, W, P, ns, t_new, pos0, past):
    m = x2d.shape[0]
    t_rows = m // ns
    decode = past is not None
    tm = _tile(m, (256, 128))
    conf, xbc, sbq, sbkv, sbkvb, nsaq, ssz, nsakv, misc = _inproj(x2d, P["g_mix"], P["w_tok"], TOK_WIDTHS, TOK_DTYPES, tm)

    pos = pos0 + jnp.arange(t_rows)
    tp = _tile(t_rows, (256, 128, 8))
    qn, rows, win, cmpk, cmpv, selkv, winkv = _nsa_prep(nsaq, nsakv, _rope_tables(pos), W["nsa_q_gain"][l],
                                                        W["nsa_k_gain"][l], ns, tp)
    w1a, w1b, cvec, w2bd = P["cmp"]
    if not decode:
        nchunk = t_rows // CMP_STRIDE
        kvc = _nsa_compress(cmpk.reshape(ns * nchunk, 2048), cmpv.reshape(ns * nchunk, 2048), w1a, w1b, cvec, w2bd,
                            _tile(ns * nchunk, (512, 256, 128)))
        tq = 128
        o_nsa = _nsa_attn(qn, misc, kvc.reshape(ns, nchunk, 256), selkv.reshape(ns, t_rows, 256),
                          winkv.reshape(ns, t_rows, 256), ns, tq, 512, 0, 0, WINDOW + tq, True)
        nsa_win = win.reshape(ns, t_rows, 2, NSA_KV_HEADS, HEAD_DIM)[:, t_rows - min(WINDOW, t_rows):]
    else:
        pool, tbl, past_win, layer_off = past["nsa_pool"], past["tbl"], past["nsa_win"], past["nsa_off"]
        past_len = tbl.shape[1] * PAGE
        new_rows = jnp.zeros((ns, PAGE, 512), f32).at[:, :t_new].set(rows.reshape(ns, t_rows, 512)[:, :t_new])
        pg = 8
        ck, cv, skv = _page_copy(pool, tbl, new_rows, layer_off, pg)
        t_pad = ck.shape[1]
        nchunk = t_pad // CMP_STRIDE
        kvc = _nsa_compress(ck.reshape(ns * nchunk, 2048), cv.reshape(ns * nchunk, 2048), w1a, w1b, cvec, w2bd,
                            _tile(ns * nchunk, (1024, 512, 256, 128)))
        win_new = win.reshape(ns, t_rows, 2, NSA_KV_HEADS, HEAD_DIM)[:, :t_new]
        w_all = jnp.concatenate([past_win, win_new], axis=1)
        wl = w_all.shape[1]
        wl_pad = -(-wl // 8) * 8
        wkv = jnp.zeros((ns, wl_pad, 256), bf16).at[:, :wl].set(w_all.reshape(ns, wl, 256).astype(bf16))
        o_nsa = _nsa_attn(qn, misc, kvc.reshape(ns, nchunk, 256), skv, wkv, ns, t_rows, 512, past_len,
                          past_len - past_win.shape[1], wl_pad, False)
        nsa_win = w_all[:, t_new:]
    nsa_rows = rows.reshape(ns, t_rows, 4, NSA_KV_HEADS, HEAD_DIM)[:, :t_new]

    if not decode:
        prev8 = jnp.zeros((ns, 8, xbc.shape[1]), f32)
        h0 = jnp.zeros((ns, SSD_HEADS * SSD_P, SSD_N), f32)
        lc = min(SSD_CHUNK, t_rows)
    else:
        prev8 = jnp.zeros((ns, 8, xbc.shape[1]), f32).at[:, 8 - (SSD_CONV - 1):].set(past["ssd_conv"])
        h0 = past["ssm"].reshape(ns, SSD_HEADS * SSD_P, SSD_N)
        lc = t_rows
    o_ssd, h_new, cst = _ssd(xbc, ssz, misc, prev8, h0, W["ssd_conv_w"][l], W["ssd_conv_b"][l], W["ssd_dt_bias"][l],
                             W["ssd_a_log"][l], W["ssd_d"][l], W["ssd_norm_g"][l], ns, lc, t_new if decode else lc)
    ssm = h_new.reshape(ns, SSD_HEADS, SSD_P, SSD_N)
    ssd_conv = cst[:, 8 - (SSD_CONV - 1):]

    if not decode:
        o_sb = _sb_prefill(sbq, sbkvb.reshape(ns, t_rows, 1024), ns, _tile(t_rows, (256, 128)))
    else:
        q4 = sbq.reshape(ns, t_rows, SB_HEADS, HEAD_DIM).transpose(0, 2, 1, 3)
        eye = jnp.eye(SB_HEADS, dtype=bf16)
        qbd = (q4[:, :, :, None, :] * eye[None, :, None, :, None]).reshape(ns, SB_HEADS * t_rows, SB_HEADS * HEAD_DIM)
        kvn = jnp.zeros((ns, PAGE, 1024), bf16).at[:, :t_new].set(sbkvb.reshape(ns, t_rows, 1024)[:, :t_new])
        o_sb = _sb_decode(qbd, kvn[:, :, :512], kvn[:, :, 512:], past["sb_pool"], past["tbl"], past["sb_off"], 8, t_new)
        o_sb = o_sb.reshape(ns * t_rows, 512)
    sb_rows = sbkv.reshape(ns, t_rows, 2, SB_HEADS, HEAD_DIM)[:, :t_new]

    ch = conf.shape[1] // 2
    if not decode:
        prev32 = jnp.zeros((ns, 32, ch), f32)
        lcc = _tile(t_rows, (256, 128))
    else:
        prev32 = jnp.zeros((ns, 32, ch), f32).at[:, 32 - (CONF_W - 1):].set(past["conf_conv"])
        lcc = t_rows
    o_conf, ccs = _conformer(conf, prev32, W["conf_glu_b"][l], W["conf_conv_w"][l], W["conf_conv_b"][l],
                             W["conf_ln_g"][l], W["conf_ln_b"][l], ns, lcc, t_new if decode else lcc)
    conf_conv = ccs[:, 32 - (CONF_W - 1):]

    x_mid = _merge(x2d, P["g_mix"], P["wg"], P["bg"], (o_nsa, o_ssd, o_sb, o_conf), P["wb"], P["wo"], tm)
    return x_mid, (nsa_rows, nsa_win, sb_rows, ssm, ssd_conv, conf_conv)


def kernel(x_prompt, x_sample, cache_nsa_kv, cache_nsa_win, cache_sb_kv, state_ssm, state_ssd_conv, state_conf_conv, page_table, g_mix, w_in, nsa_q_gain, nsa_k_gain, nsa_cmp_pe, nsa_cmp_w1, nsa_cmp_w2, ssd_conv_w, ssd_conv_b, ssd_dt_bias, ssd_a_log, ssd_d, ssd_norm_g, conf_glu_b, conf_conv_w, conf_conv_b, conf_ln_g, conf_ln_b, w_branch, b_gate, w_out, g_ffn, ff_w_gate, ff_w_up, ff_w_down, moe_w_router, moe_b_router, moe_w_gate, moe_w_up, moe_w_down):
    W = dict(g_mix=g_mix, w_in=w_in, nsa_q_gain=nsa_q_gain, nsa_k_gain=nsa_k_gain, nsa_cmp_pe=nsa_cmp_pe,
             nsa_cmp_w1=nsa_cmp_w1, nsa_cmp_w2=nsa_cmp_w2, ssd_conv_w=ssd_conv_w, ssd_conv_b=ssd_conv_b,
             ssd_dt_bias=ssd_dt_bias, ssd_a_log=ssd_a_log, ssd_d=ssd_d, ssd_norm_g=ssd_norm_g, conf_glu_b=conf_glu_b,
             conf_conv_w=conf_conv_w, conf_conv_b=conf_conv_b, conf_ln_g=conf_ln_g, conf_ln_b=conf_ln_b,
             w_branch=w_branch, b_gate=b_gate, w_out=w_out, g_ffn=g_ffn)
    depth = w_in.shape[0]
    nb, t, d = x_prompt.shape
    ns, ts, _ = x_sample.shape
    n_phys = cache_nsa_kv.shape[1]
    past_len = page_table.shape[1] * PAGE
    ts_pad = -(-ts // 8) * 8
    nsa_pool = cache_nsa_kv.reshape(depth * n_phys, PAGE, 4 * NSA_KV_HEADS * HEAD_DIM)
    sb_pool = cache_sb_kv.reshape(depth * n_phys, PAGE, 2 * SB_HEADS * HEAD_DIM)

    xp = x_prompt.reshape(nb * t, d)
    xs = jnp.zeros((ns, ts_pad, d), f32).at[:, :ts].set(x_sample).reshape(ns * ts_pad, d)
    st_p, st_s = [], []
    for l in range(depth):
        P = _prep_weights(W, l)
        xp, sp = _token_mixers(xp, l, W, P, nb, t, 0, None)
        past = dict(nsa_pool=nsa_pool, sb_pool=sb_pool, tbl=page_table, nsa_off=l * n_phys, sb_off=l * n_phys,
                    nsa_win=cache_nsa_win[l], ssm=state_ssm[l], ssd_conv=state_ssd_conv[l], conf_conv=state_conf_conv[l])
        xs, ss = _token_mixers(xs, l, W, P, ns, ts, past_len, past)
        i = l // 2
        if l % 2 == 0:
            wg, wu, wd = ff_w_gate[i].astype(bf16), ff_w_up[i].astype(bf16), ff_w_down[i].astype(bf16)
            tf = _tile(wg.shape[1], (1408, 512, 256, 128))
            xp = _ffn(xp, P["g_ffn"], wg, wu, wd, _tile(xp.shape[0], (512, 256, 128)), tf)
            xs = _ffn(xs, P["g_ffn"], wg, wu, wd, _tile(xs.shape[0], (512, 256, 128)), tf)
        else:
            wg, wu, wd = moe_w_gate[i].astype(bf16), moe_w_up[i].astype(bf16), moe_w_down[i].astype(bf16)
            tf = _tile(wg.shape[2], (1792, 512, 256, 128))
            xp = _moe(xp, P["g_ffn"], moe_w_router[i], moe_b_router[i], wg, wu, wd, _tile(xp.shape[0], (512, 256, 128)), 512, tf)
            xs = _moe(xs, P["g_ffn"], moe_w_router[i], moe_b_router[i], wg, wu, wd, _tile(xs.shape[0], (512, 256, 128)), 128, tf)
        st_p.append(sp)
        st_s.append(ss)
    y_prompt = xp.reshape(nb, t, d)
    y_sample = xs.reshape(ns, ts_pad, d)[:, :ts]
    nsa_kv_p, nsa_win_p, sb_kv_p, ssm_p, ssd_conv_p, conf_conv_p = [jnp.stack(s, axis=0) for s in zip(*st_p)]
    nsa_kv_s, nsa_win_s, sb_kv_s, ssm_s, ssd_conv_s, conf_conv_s = [jnp.stack(s, axis=0) for s in zip(*st_s)]
    return (y_prompt, y_sample, nsa_kv_p, nsa_kv_s, nsa_win_p, nsa_win_s, sb_kv_p, sb_kv_s,
            ssm_p, ssm_s, ssd_conv_p, ssd_conv_s, conf_conv_p, conf_conv_s)
```

```python
import functools
import math

import jax
import jax.numpy as jnp
from jax import lax
from jax.experimental import pallas as pl
from jax.experimental.pallas import tpu as pltpu

f32 = jnp.float32
bf16 = jnp.bfloat16
i32 = jnp.int32

HEAD_DIM = 64
ROPE_DIM = 16
ROPE_THETA = 500000.0
NORM_EPS = 1e-6
LN_EPS = 1e-5
PAGE = 128
NSA_HEADS = 8
NSA_KV_HEADS = 2
NSA_GROUP = 4
CMP_LEN = 32
CMP_STRIDE = 16
CMP_HIDDEN = 128
SEL_LEN = 64
TOP_N = 16
WINDOW = 512
SSD_HEADS = 8
SSD_P = 64
SSD_N = 128
SSD_GROUPS = 2
SSD_CONV = 4
SSD_CHUNK = 128
SB_HEADS = 8
CONF_W = 31
N_EXPERTS = 8
TOP_K = 2

V7X_VMEM_BYTES = 64 * 1024 * 1024
VMEM_LIMIT = V7X_VMEM_BYTES - 8 * 1024 * 1024
LANES = 128
BIG = 1e30

MISC_GATE0 = 0
MISC_DT0 = 24


def _cparams(*sem):
    return pltpu.CompilerParams(dimension_semantics=sem, vmem_limit_bytes=VMEM_LIMIT)


def _dot(a, b):
    return jnp.dot(a, b, preferred_element_type=f32)


def _dot_nt(a, b):
    return lax.dot_general(a, b, (((1,), (1,)), ((), ())), preferred_element_type=f32)


def _dot_tn(a, b):
    return lax.dot_general(a, b, (((0,), (0,)), ((), ())), preferred_element_type=f32)


def _dot_hi(a, b):
    return jnp.dot(a, b, preferred_element_type=f32, precision=lax.Precision.HIGHEST)


def _split_dot(x, u):
    hi = x.astype(bf16)
    lo = (x - hi.astype(f32)).astype(bf16)
    return _dot(hi, u) + _dot(lo, u)


def _sigmoid(x):
    return 1.0 / (1.0 + jnp.exp(-x))


def _silu(x):
    return x * _sigmoid(x)


def _softplus(x):
    return jnp.maximum(x, 0.0) + jnp.log1p(jnp.exp(-jnp.abs(x)))


def _iota(shape, dim):
    return lax.broadcasted_iota(i32, shape, dim)


def _inproj_kernel(x_ref, g_ref, w_ref, *o_refs, widths):
    x = x_ref[...]
    h = (x * lax.rsqrt(jnp.mean(x * x, axis=-1, keepdims=True) + NORM_EPS) * g_ref[...]).astype(bf16)
    off = 0
    for o_ref, wd in zip(o_refs, widths):
        o_ref[...] = _dot(h, w_ref[:, off:off + wd]).astype(o_ref.dtype)
        off += wd


def _inproj(x, g, w, widths, dtypes, tm):
    m, d = x.shape
    return pl.pallas_call(
        functools.partial(_inproj_kernel, widths=widths),
        grid=(m // tm,),
        in_specs=[pl.BlockSpec((tm, d), lambda i: (i, 0)),
                  pl.BlockSpec((1, d), lambda i: (0, 0)),
                  pl.BlockSpec(w.shape, lambda i: (0, 0))],
        out_specs=[pl.BlockSpec((tm, wd), lambda i: (i, 0)) for wd in widths],
        out_shape=[jax.ShapeDtypeStruct((m, wd), dt) for wd, dt in zip(widths, dtypes)],
        compiler_params=_cparams("parallel"),
        name="inproj",
    )(x, g, w)


def _nsa_prep_kernel(q_ref, kv_ref, cos_ref, s1_ref, s2_ref, qg_ref, kg_ref,
                     qn_ref, rows_ref, win_ref, cmpk_ref, cmpv_ref, selkv_ref, winkv_ref):
    cs, s1, s2 = cos_ref[...], s1_ref[...], s2_ref[...]
    lo = _iota((1, LANES), 1) < HEAD_DIM

    def normrope(x, gain):
        sq = x * x
        m0 = jnp.sum(jnp.where(lo, sq, 0.0), axis=-1, keepdims=True) * (1.0 / HEAD_DIM)
        m1 = jnp.sum(jnp.where(lo, 0.0, sq), axis=-1, keepdims=True) * (1.0 / HEAD_DIM)
        y = x * jnp.where(lo, lax.rsqrt(m0 + NORM_EPS), lax.rsqrt(m1 + NORM_EPS)) * gain
        return y * cs + pltpu.roll(y, LANES - ROPE_DIM // 2, 1) * s1 + pltpu.roll(y, ROPE_DIM // 2, 1) * s2

    qg = qg_ref[...]
    for j in range(NSA_HEADS // 2):
        sl = slice(j * LANES, (j + 1) * LANES)
        qn_ref[:, sl] = (normrope(q_ref[:, sl], qg) * (HEAD_DIM ** -0.5)).astype(bf16)
    kn = [normrope(kv_ref[:, (2 * j) * LANES:(2 * j + 1) * LANES], kg_ref[j:j + 1, :]) for j in range(3)]
    vv = [kv_ref[:, (2 * j + 1) * LANES:(2 * j + 2) * LANES] for j in range(3)]
    rows_ref[:, 0 * LANES:1 * LANES] = kn[0]
    rows_ref[:, 1 * LANES:2 * LANES] = vv[0]
    rows_ref[:, 2 * LANES:3 * LANES] = kn[1]
    rows_ref[:, 3 * LANES:4 * LANES] = vv[1]
    win_ref[:, 0:LANES] = kn[2]
    win_ref[:, LANES:2 * LANES] = vv[2]
    cmpk_ref[...] = kn[0].astype(bf16)
    cmpv_ref[...] = vv[0].astype(bf16)
    selkv_ref[:, 0:LANES] = kn[1].astype(bf16)
    selkv_ref[:, LANES:2 * LANES] = vv[1].astype(bf16)
    winkv_ref[:, 0:LANES] = kn[2].astype(bf16)
    winkv_ref[:, LANES:2 * LANES] = vv[2].astype(bf16)


def _rope_tables(pos):
    half = ROPE_DIM // 2
    inv = ROPE_THETA ** (-jnp.arange(half, dtype=f32) / half)
    ang = pos.astype(f32)[:, None] * inv[None, :]
    cos, sin = jnp.cos(ang), jnp.sin(ang)
    n = pos.shape[0]
    z = lambda k: jnp.zeros((n, k), f32)
    c64 = jnp.concatenate([cos, cos, jnp.ones((n, HEAD_DIM - ROPE_DIM), f32)], axis=1)
    s1 = jnp.concatenate([-sin, z(HEAD_DIM - half)], axis=1)
    s2 = jnp.concatenate([z(half), sin, z(HEAD_DIM - ROPE_DIM)], axis=1)
    return tuple(jnp.tile(t, (1, 2)) for t in (c64, s1, s2))


def _nsa_prep(q, kv, tables, q_gain, k_gain, nb, tm):
    m = q.shape[0]
    nt = m // nb // tm
    row = lambda b, i: (b * nt + i, 0)
    tab = lambda b, i: (i, 0)
    cst = lambda b, i: (0, 0)
    qg = jnp.tile(q_gain.reshape(1, HEAD_DIM), (1, 2))
    kg = jnp.tile(k_gain.reshape(3, HEAD_DIM), (1, 2))
    kg = jnp.concatenate([kg, jnp.zeros((5, LANES), f32)], axis=0)
    widths = (512, 512, 256, 128, 128, 256, 256)
    dts = (bf16, f32, f32, bf16, bf16, bf16, bf16)
    return pl.pallas_call(
        _nsa_prep_kernel,
        grid=(nb, nt),
        in_specs=[pl.BlockSpec((tm, 512), row), pl.BlockSpec((tm, 768), row),
                  pl.BlockSpec((tm, LANES), tab), pl.BlockSpec((tm, LANES), tab), pl.BlockSpec((tm, LANES), tab),
                  pl.BlockSpec((1, LANES), cst), pl.BlockSpec((8, LANES), cst)],
        out_specs=[pl.BlockSpec((tm, w), row) for w in widths],
        out_shape=[jax.ShapeDtypeStruct((m, w), d) for w, d in zip(widths, dts)],
        compiler_params=_cparams("parallel", "parallel"),
        name="nsa_prep",
    )(q, kv, *tables, qg, kg)


def _nsa_cmp_kernel(ck_ref, cv_ref, nk_ref, nv_ref, w1a_ref, w1b_ref, c_ref, w2_ref, o_ref, sh_ref, *, tc):
    for kind, (x_ref, n_ref) in enumerate(((ck_ref, nk_ref), (cv_ref, nv_ref))):
        x = x_ref[...]
        pa = _dot(x, w1a_ref[kind])
        sh_ref[pl.ds(0, tc), :] = _dot(x, w1b_ref[kind])
        sh_ref[pl.ds(tc, 8), :] = _dot(n_ref[...], w1b_ref[kind])
        hid = pa + sh_ref[pl.ds(1, tc), :] + c_ref[kind]
        o_ref[:, kind * LANES:(kind + 1) * LANES] = _dot(jax.nn.gelu(hid).astype(bf16), w2_ref[kind])


def _nsa_compress(ck, cv, w1a, w1b, cvec, w2bd, tc):
    r = ck.shape[0]
    nxt = lambda i: (jnp.minimum((i + 1) * (tc // 8), r // 8 - 1), 0)
    full = lambda a: pl.BlockSpec(a.shape, lambda i: (0,) * a.ndim)
    return pl.pallas_call(
        functools.partial(_nsa_cmp_kernel, tc=tc),
        grid=(r // tc,),
        in_specs=[pl.BlockSpec((tc, 2048), lambda i: (i, 0)), pl.BlockSpec((tc, 2048), lambda i: (i, 0)),
                  pl.BlockSpec((8, 2048), nxt), pl.BlockSpec((8, 2048), nxt),
                  full(w1a), full(w1b), full(cvec), full(w2bd)],
        out_specs=pl.BlockSpec((tc, 256), lambda i: (i, 0)),
        out_shape=jax.ShapeDtypeStruct((r, 256), f32),
        scratch_shapes=[pltpu.VMEM((tc + 8, 256), f32)],
        compiler_params=_cparams("parallel"),
        name="nsa_compress",
    )(ck, cv, ck, cv, w1a, w1b, cvec, w2bd)


def _cmp_weights(pe, w1, w2):
    def two_head(w):
        w = w.reshape(CMP_STRIDE, 1, HEAD_DIM, 1, CMP_HIDDEN)
        eye = jnp.eye(2, dtype=f32).reshape(1, 2, 1, 2, 1)
        return (w * eye).reshape(CMP_STRIDE * 2 * HEAD_DIM, 2 * CMP_HIDDEN)
    half = CMP_STRIDE * HEAD_DIM
    w1a = jnp.stack([two_head(w1[k, :half]) for k in range(2)]).astype(bf16)
    w1b = jnp.stack([two_head(w1[k, half:]) for k in range(2)]).astype(bf16)
    c = jnp.stack([jnp.tile(jnp.dot(pe[k].reshape(1, -1), w1[k], precision=lax.Precision.HIGHEST), (1, 2))
                   for k in range(2)])
    eye2 = jnp.eye(2, dtype=f32)
    w2bd = jnp.stack([jnp.kron(eye2, w2[k]) for k in range(2)]).astype(bf16)
    return w1a, w1b, c, w2bd


def _masked_softmax(s, mask):
    s = jnp.where(mask, s, -BIG)
    m = jnp.max(s, axis=-1, keepdims=True)
    e = jnp.where(mask, jnp.exp(s - m), 0.0)
    den = jnp.sum(e, axis=-1, keepdims=True)
    return e / jnp.where(den > 0, den, 1.0)


def _nsa_attn_kernel(q_ref, misc_ref, kvc_ref, sel_ref, win_ref, cover_ref, o_ref, *,
                     tq, tk, qbase, nch, nselp, win_base, win_len, win_slide):
    i = pl.program_id(1)
    q0 = qbase + i * tq
    g4 = NSA_GROUP
    qpos = q0 + _iota((tq, 1), 0)
    qpos4 = jnp.concatenate([qpos] * g4, axis=0)
    gates = _sigmoid(misc_ref[...])
    cend = _iota((1, nch), 1) * CMP_STRIDE + (CMP_LEN - 1)
    blk = _iota((1, nselp), 1)
    blkf = blk.astype(f32)
    cur = jnp.right_shift(qpos, 6)
    forced = (blk == 0) | (blk == cur) | (blk == cur - 1)
    valid = blk * SEL_LEN <= qpos
    cover = cover_ref[...]
    n_t = (q0 + tq - 1) // tk + 1
    if win_slide:
        w0 = pl.multiple_of(jnp.maximum(q0 - WINDOW, 0), 128)
    else:
        w0 = 0
    kwpos = win_base + w0 + _iota((1, win_len), 1)
    dpos = qpos4 - kwpos
    m_w = (dpos >= 0) & (dpos < WINDOW)

    nkv = NSA_KV_HEADS
    hs = [slice(kvh * HEAD_DIM, (kvh + 1) * HEAD_DIM) for kvh in range(nkv)]
    vs = [slice(LANES + kvh * HEAD_DIM, LANES + (kvh + 1) * HEAD_DIM) for kvh in range(nkv)]
    qg = [jnp.concatenate([q_ref[:, (kvh * g4 + g) * HEAD_DIM:(kvh * g4 + g + 1) * HEAD_DIM] for g in range(g4)],
                          axis=0) for kvh in range(nkv)]

    o_c, v0 = [], []
    for kvh in range(nkv):
        kc = kvc_ref[0, :, hs[kvh]].astype(bf16)
        vc = kvc_ref[0, :, vs[kvh]].astype(bf16)
        p_c = _masked_softmax(_dot_nt(qg[kvh], kc), cend <= qpos4)
        o_c.append(_dot(p_c.astype(bf16), vc))
        psum = p_c[0:tq]
        for g in range(1, g4):
            psum = psum + p_c[g * tq:(g + 1) * tq]
        imp = _split_dot(psum, cover)
        v0.append(jnp.where(forced, BIG, jnp.where(valid, imp, -BIG)))

    def pick(_, carry):
        v, selm = carry
        m = jnp.max(v, axis=-1, keepdims=True)
        first = jnp.min(jnp.where(v == m, blkf, float(nselp)), axis=-1, keepdims=True)
        hit = blkf == first
        return jnp.where(hit, -3.0e38, v), jnp.where(hit, 1.0, selm)

    _, selm = lax.fori_loop(0, TOP_N, pick, (jnp.concatenate(v0, axis=0), jnp.zeros((nkv * tq, nselp), f32)))
    selb = selm.astype(bf16)

    def tile(t, carry):
        k0 = pl.multiple_of(t * tk, tk)
        kpos = k0 + _iota((1, tk), 1)
        expand = jnp.where(_iota((nselp, tk), 0) == jnp.right_shift(k0 + _iota((nselp, tk), 1), 6), 1.0, 0.0).astype(bf16)
        okf = _dot(selb, expand)
        out = []
        for kvh in range(nkv):
            m_i, l_i, acc = carry[kvh]
            kt = sel_ref[0, pl.ds(k0, tk), hs[kvh]]
            vt = sel_ref[0, pl.ds(k0, tk), vs[kvh]]
            bias = jnp.where(kpos <= qpos, (okf[kvh * tq:(kvh + 1) * tq] - 1.0) * BIG, -BIG)
            s = _dot_nt(qg[kvh], kt) + jnp.concatenate([bias] * g4, axis=0)
            m_new = jnp.maximum(m_i, jnp.max(s, axis=-1, keepdims=True))
            alpha = jnp.exp(m_i - m_new)
            p = jnp.exp(s - m_new)
            l_new = alpha * l_i + jnp.sum(p, axis=-1, keepdims=True)
            out.append((m_new, l_new, alpha * acc + _dot(p.astype(bf16), vt)))
        return tuple(out)

    init = tuple((jnp.full((g4 * tq, 1), -BIG, f32), jnp.zeros((g4 * tq, 1), f32),
                  jnp.zeros((g4 * tq, HEAD_DIM), f32)) for _ in range(nkv))
    res = lax.fori_loop(0, n_t, tile, init)

    for kvh in range(nkv):
        _, l_s, acc_s = res[kvh]
        o_s = acc_s / jnp.where(l_s > 0, l_s, 1.0)
        kw = win_ref[0, pl.ds(w0, win_len), hs[kvh]]
        vw = win_ref[0, pl.ds(w0, win_len), vs[kvh]]
        p_w = _masked_softmax(_dot_nt(qg[kvh], kw), m_w)
        o_w = _dot(p_w.astype(bf16), vw)
        for g in range(g4):
            h = kvh * g4 + g
            rs = slice(g * tq, (g + 1) * tq)
            o = (gates[:, 3 * h:3 * h + 1] * o_c[kvh][rs] + gates[:, 3 * h + 1:3 * h + 2] * o_s[rs]
                 + gates[:, 3 * h + 2:3 * h + 3] * o_w[rs])
            o_ref[:, h * HEAD_DIM:(h + 1) * HEAD_DIM] = o.astype(o_ref.dtype)


def _nsa_attn(qn, misc, kvc, selkv, winkv, ns, tq, tk, qbase, win_base, win_len, win_slide):
    m = qn.shape[0]
    nq = m // ns // tq
    nch = kvc.shape[1]
    t_keys = selkv.shape[1]
    nselp = -(-(t_keys // SEL_LEN) // LANES) * LANES
    ci = jnp.arange(nch)[:, None] * CMP_STRIDE
    cj = jnp.arange(nselp)[None, :] * SEL_LEN
    cover = ((ci < cj + SEL_LEN) & (ci + CMP_LEN > cj)).astype(bf16)
    row = lambda s, i: (s * nq + i, 0)
    seq = lambda s, i: (s, 0, 0)
    return pl.pallas_call(
        functools.partial(_nsa_attn_kernel, tq=tq, tk=tk, qbase=qbase, nch=nch, nselp=nselp,
                          win_base=win_base, win_len=win_len, win_slide=win_slide),
        grid=(ns, nq),
        in_specs=[pl.BlockSpec((tq, 512), row), pl.BlockSpec((tq, LANES), row),
                  pl.BlockSpec((1, nch, 256), seq), pl.BlockSpec((1, t_keys, 256), seq),
                  pl.BlockSpec((1, winkv.shape[1], 256), seq),
                  pl.BlockSpec((nch, nselp), lambda s, i: (0, 0))],
        out_specs=pl.BlockSpec((tq, 512), row),
        out_shape=jax.ShapeDtypeStruct((m, 512), bf16),
        compiler_params=_cparams("parallel", "arbitrary"),
        name="nsa_attn",
    )(qn, misc, kvc, selkv, winkv, cover)


def _page_copy_kernel(tbl_ref, *refs, pg):
    del tbl_ref
    pages, new_ref = refs[:pg], refs[pg]
    cmpk_ref, cmpv_ref, selkv_ref = refs[pg + 1:]
    g = pl.program_id(1)
    last = pl.num_programs(1) - 1

    @pl.when(g < last)
    def _():
        for j in range(pg):
            rs = pl.ds(j * PAGE, PAGE)
            cmpk_ref[0, rs, :] = pages[j][0, :, 0:LANES].astype(bf16)
            cmpv_ref[0, rs, :] = pages[j][0, :, LANES:2 * LANES].astype(bf16)
            selkv_ref[0, rs, :] = pages[j][0, :, 2 * LANES:4 * LANES].astype(bf16)

    @pl.when(g == last)
    def _():
        cmpk_ref[0] = jnp.zeros((pg * PAGE, LANES), bf16)
        cmpv_ref[0] = jnp.zeros((pg * PAGE, LANES), bf16)
        selkv_ref[0] = jnp.zeros((pg * PAGE, 2 * LANES), bf16)
        cmpk_ref[0, pl.ds(0, PAGE), :] = new_ref[0, :, 0:LANES].astype(bf16)
        cmpv_ref[0, pl.ds(0, PAGE), :] = new_ref[0, :, LANES:2 * LANES].astype(bf16)
        selkv_ref[0, pl.ds(0, PAGE), :] = new_ref[0, :, 2 * LANES:4 * LANES].astype(bf16)


def _page_copy(pool, tbl, new_rows, layer_off, pg):
    s, n_pages = tbl.shape
    ng = n_pages // pg
    t_pad = (ng + 1) * pg * PAGE

    def pspec(j):
        return pl.BlockSpec((1, PAGE, 512), lambda b, g, t: (layer_off + t[b, jnp.minimum(g, ng - 1) * pg + j], 0, 0))
    out = lambda w: pl.BlockSpec((1, pg * PAGE, w), lambda b, g, t: (b, g, 0))
    gs = pltpu.PrefetchScalarGridSpec(
        num_scalar_prefetch=1, grid=(s, ng + 1),
        in_specs=[pspec(j) for j in range(pg)] + [pl.BlockSpec((1, PAGE, 512), lambda b, g, t: (b, 0, 0))],
        out_specs=[out(LANES), out(LANES), out(2 * LANES)])
    return pl.pallas_call(
        functools.partial(_page_copy_kernel, pg=pg),
        grid_spec=gs,
        out_shape=[jax.ShapeDtypeStruct((s, t_pad, LANES), bf16), jax.ShapeDtypeStruct((s, t_pad, LANES), bf16),
                   jax.ShapeDtypeStruct((s, t_pad, 2 * LANES), bf16)],
        compiler_params=_cparams("parallel", "arbitrary"),
        name="nsa_page_copy",
    )(tbl, *([pool] * pg), new_rows)


def _ssd_kernel(xbc_ref, z_ref, misc_ref, prev_ref, h0_ref, cw_ref, cb_ref, dtb_ref, alog_ref, dsk_ref, ng_ref,
                y_ref, hn_ref, cs_ref, xs_ref, hst_ref, *, lc, t_valid):
    c = pl.program_id(1)
    inner = SSD_HEADS * SSD_P

    @pl.when(c == 0)
    def _():
        xs_ref[pl.ds(0, 8), :] = prev_ref[0]
        hst_ref[...] = h0_ref[0]

    xs_ref[pl.ds(8, lc), :] = xbc_ref[...]
    conv = cb_ref[...]
    for k in range(SSD_CONV):
        conv = conv + cw_ref[k:k + 1, :] * xs_ref[pl.ds(8 - (SSD_CONV - 1) + k, lc), :]
    act = _silu(conv)
    cs_ref[0] = xs_ref[pl.ds(t_valid, 8), :]
    xs_ref[pl.ds(0, 8), :] = xs_ref[pl.ds(lc, 8), :]

    rows = _iota((lc, 1), 0)
    lane = _iota((1, LANES), 1)
    dt_lane = (lane >= MISC_DT0) & (lane < MISC_DT0 + SSD_HEADS)
    dt = _softplus(misc_ref[...] + dtb_ref[...])
    dt = jnp.where(dt_lane & (rows < t_valid), dt, 0.0)
    dta = dt * jnp.where(dt_lane, -jnp.exp(alog_ref[...]), 0.0)
    tt = _iota((lc, lc), 0)
    ss = _iota((lc, lc), 1)
    tril = tt >= ss
    la = _dot_hi(jnp.where(tril, 1.0, 0.0), dta)
    z = z_ref[...]
    dsk = dsk_ref[...]
    ys = []
    for g in range(SSD_GROUPS):
        bg = act[:, inner + g * SSD_N: inner + (g + 1) * SSD_N]
        cg = act[:, inner + (SSD_GROUPS + g) * SSD_N: inner + (SSD_GROUPS + g + 1) * SSD_N]
        bgb = bg.astype(bf16)
        cb = _dot_nt(cg.astype(bf16), bgb)
        for hh in range(SSD_HEADS // SSD_GROUPS):
            h = g * (SSD_HEADS // SSD_GROUPS) + hh
            ln = MISC_DT0 + h
            col = la[:, ln:ln + 1]
            row = jnp.sum(jnp.where(tt <= ss, dta[:, ln:ln + 1], 0.0), axis=0, keepdims=True)
            decay = jnp.exp(jnp.where(tril, col - row, -jnp.inf))
            xh = act[:, h * SSD_P:(h + 1) * SSD_P]
            xdt = xh * dt[:, ln:ln + 1]
            y = _dot((cb * decay).astype(bf16), xdt.astype(bf16))
            hs = hst_ref[h * SSD_P:(h + 1) * SSD_P, :]
            y = y + _dot_nt((cg * jnp.exp(col)).astype(bf16), hs.astype(bf16))
            last = la[lc - 1:lc, ln:ln + 1]
            sck = _dot_tn((xdt * jnp.exp(last - col)).astype(bf16), bgb)
            hst_ref[h * SSD_P:(h + 1) * SSD_P, :] = hs * jnp.exp(last) + sck
            y = y + dsk[:, h * SSD_P:(h + 1) * SSD_P] * xh
            ys.append(y * _silu(z[:, h * SSD_P:(h + 1) * SSD_P]))
    gw = inner // SSD_GROUPS
    hpg = SSD_HEADS // SSD_GROUPS
    for g in range(SSD_GROUPS):
        yg = jnp.concatenate(ys[g * hpg:(g + 1) * hpg], axis=1)
        yn = yg * lax.rsqrt(jnp.mean(yg * yg, axis=-1, keepdims=True) + NORM_EPS) * ng_ref[:, g * gw:(g + 1) * gw]
        y_ref[:, g * gw:(g + 1) * gw] = yn.astype(y_ref.dtype)

    @pl.when(c == pl.num_programs(1) - 1)
    def _():
        hn_ref[0] = hst_ref[...]


def _ssd(xbc, z, misc, prev8, h0, conv_w, conv_b, dt_bias, a_log, d_skip, norm_g, ns, lc, t_valid):
    m = xbc.shape[0]
    nc = m // ns // lc
    inner = SSD_HEADS * SSD_P
    row = lambda s, c: (s * nc + c, 0)
    seq = lambda s, c: (s, 0, 0)
    cst = lambda s, c: (0, 0)
    cw = jnp.concatenate([conv_w, jnp.zeros((8 - SSD_CONV, conv_w.shape[1]), f32)], axis=0)
    lanepad = lambda v: jnp.zeros((1, LANES), f32).at[0, MISC_DT0:MISC_DT0 + SSD_HEADS].set(v)
    return pl.pallas_call(
        functools.partial(_ssd_kernel, lc=lc, t_valid=t_valid),
        grid=(ns, nc),
        in_specs=[pl.BlockSpec((lc, 1024), row), pl.BlockSpec((lc, inner), row), pl.BlockSpec((lc, LANES), row),
                  pl.BlockSpec((1, 8, 1024), seq), pl.BlockSpec((1, inner, SSD_N), seq),
                  pl.BlockSpec((8, 1024), cst), pl.BlockSpec((1, 1024), cst),
                  pl.BlockSpec((1, LANES), cst), pl.BlockSpec((1, LANES), cst),
                  pl.BlockSpec((1, inner), cst), pl.BlockSpec((1, inner), cst)],
        out_specs=[pl.BlockSpec((lc, inner), row), pl.BlockSpec((1, inner, SSD_N), seq), pl.BlockSpec((1, 8, 1024), seq)],
        out_shape=[jax.ShapeDtypeStruct((m, inner), bf16), jax.ShapeDtypeStruct((ns, inner, SSD_N), f32),
                   jax.ShapeDtypeStruct((ns, 8, 1024), f32)],
        scratch_shapes=[pltpu.VMEM((lc + 8, 1024), f32), pltpu.VMEM((inner, SSD_N), f32)],
        compiler_params=_cparams("parallel", "arbitrary"),
        name="ssd",
    )(xbc, z, misc, prev8, h0, cw, conv_b.reshape(1, -1), lanepad(dt_bias), lanepad(a_log),
      jnp.repeat(d_skip, SSD_P).reshape(1, -1), norm_g.reshape(1, -1))


def _sb_tile(z, v, carry, u, mask):
    lk = -_softplus(z)
    lkm = lk if mask is None else jnp.where(mask, lk, 0.0)
    after = _split_dot(lkm, u) + carry
    w = jnp.exp(z + lk + after)
    if mask is not None:
        w = jnp.where(mask, w, 0.0)
    return _dot(w.astype(bf16), v), carry + jnp.sum(lkm, axis=-1, keepdims=True)


def _strict_upper(n):
    return jnp.where(_iota((n, n), 0) > _iota((n, n), 1), 1.0, 0.0).astype(bf16)


SB_HEADS_PER_STEP = 4


def _sb_prefill_kernel(q_ref, k_ref, v_ref, o_ref, *, tq):
    i = pl.program_id(2)
    nh = SB_HEADS_PER_STEP
    u = _strict_upper(tq)
    diag_mask = _iota((tq, tq), 1) < _iota((tq, tq), 0)
    hs = [slice(h * HEAD_DIM, (h + 1) * HEAD_DIM) for h in range(nh)]
    qs = [q_ref[:, hs[h]] for h in range(nh)]

    def tiles(k0, carries, mask):
        return [_sb_tile(_dot_nt(qs[h], k_ref[0, pl.ds(k0, tq), hs[h]]), v_ref[0, pl.ds(k0, tq), hs[h]],
                         carries[h], u, mask) for h in range(nh)]

    first = tiles(pl.multiple_of(i * tq, tq), [jnp.zeros((tq, 1), f32)] * nh, diag_mask)

    def body(j, c):
        accs, carries = c
        new = tiles(pl.multiple_of((i - 1 - j) * tq, tq), carries, None)
        return tuple(accs[h] + new[h][0] for h in range(nh)), tuple(new[h][1] for h in range(nh))

    accs, _ = lax.fori_loop(0, i, body, (tuple(a for a, _ in first), tuple(c for _, c in first)))
    for h in range(nh):
        o_ref[:, hs[h]] = accs[h].astype(o_ref.dtype)


def _sb_prefill(q, kvb, nb, tq):
    m = q.shape[0]
    t = m // nb
    nq = t // tq
    ng = SB_HEADS // SB_HEADS_PER_STEP
    wd = SB_HEADS_PER_STEP * HEAD_DIM
    return pl.pallas_call(
        functools.partial(_sb_prefill_kernel, tq=tq),
        grid=(nb, ng, nq),
        in_specs=[pl.BlockSpec((tq, wd), lambda b, h, i: (b * nq + i, h)),
                  pl.BlockSpec((1, t, wd), lambda b, h, i: (b, 0, h)),
                  pl.BlockSpec((1, t, wd), lambda b, h, i: (b, 0, ng + h))],
        out_specs=pl.BlockSpec((tq, wd), lambda b, h, i: (b * nq + i, h)),
        out_shape=jax.ShapeDtypeStruct((m, 512), bf16),
        compiler_params=_cparams("parallel", "parallel", "arbitrary"),
        name="sb_prefill",
    )(q, kvb, kvb)


def _sb_decode_kernel(tbl_ref, *refs, pg, t_new, past_len):
    del tbl_ref
    kp, vp = refs[:pg], refs[pg:2 * pg]
    qbd_ref, kn_ref, vn_ref, o_ref, acc_ref, car_ref = refs[2 * pg:]
    g = pl.program_id(1)
    qbd = qbd_ref[0]
    u = _strict_upper(PAGE)

    def tile(k, v, mask):
        z = _dot_nt(qbd, k)
        a, car = _sb_tile(z, v, car_ref[...], u, mask)
        acc_ref[...] += a
        car_ref[...] = car

    @pl.when(g == 0)
    def _():
        acc_ref[...] = jnp.zeros_like(acc_ref)
        car_ref[...] = jnp.zeros_like(car_ref)
        qi = _iota((SB_HEADS * 8, PAGE), 0) % 8
        si = _iota((SB_HEADS * 8, PAGE), 1)
        tile(kn_ref[0], vn_ref[0], (si < qi) & (si < t_new))

    for j in range(pg):
        tile(kp[j][0].astype(bf16), vp[j][0].astype(bf16), None)

    @pl.when(g == pl.num_programs(1) - 1)
    def _():
        for h in range(SB_HEADS):
            o_ref[0, :, h * HEAD_DIM:(h + 1) * HEAD_DIM] = acc_ref[h * 8:(h + 1) * 8, h * HEAD_DIM:(h + 1) * HEAD_DIM].astype(o_ref.dtype)


def _sb_decode(qbd, knew, vnew, pool, tbl, layer_off, pg, t_new):
    s, n_pages = tbl.shape
    ng = n_pages // pg

    def pspec(j, kind):
        return pl.BlockSpec((1, PAGE, 512), lambda b, g, t: (layer_off + t[b, n_pages - 1 - (g * pg + j)], 0, kind))
    seq = lambda b, g, t: (b, 0, 0)
    gs = pltpu.PrefetchScalarGridSpec(
        num_scalar_prefetch=1, grid=(s, ng),
        in_specs=[pspec(j, 0) for j in range(pg)] + [pspec(j, 1) for j in range(pg)]
        + [pl.BlockSpec((1, SB_HEADS * 8, 512), seq), pl.BlockSpec((1, PAGE, 512), seq), pl.BlockSpec((1, PAGE, 512), seq)],
        out_specs=pl.BlockSpec((1, 8, 512), seq),
        scratch_shapes=[pltpu.VMEM((SB_HEADS * 8, 512), f32), pltpu.VMEM((SB_HEADS * 8, 1), f32)])
    return pl.pallas_call(
        functools.partial(_sb_decode_kernel, pg=pg, t_new=t_new, past_len=n_pages * PAGE),
        grid_spec=gs,
        out_shape=jax.ShapeDtypeStruct((s, 8, 512), bf16),
        compiler_params=_cparams("parallel", "arbitrary"),
        name="sb_decode",
    )(tbl, *([pool] * (2 * pg)), qbd, knew, vnew)


def _conf_kernel(x_ref, prev_ref, gb_ref, cw_ref, cb_ref, lg_ref, lb_ref, o_ref, cs_ref, us_ref, *, lc, t_valid):
    c = pl.program_id(1)
    ch = o_ref.shape[-1]

    @pl.when(c == 0)
    def _():
        us_ref[pl.ds(0, 32), :] = prev_ref[0]

    v = x_ref[...] + gb_ref[...]
    us_ref[pl.ds(32, lc), :] = v[:, :ch] * _sigmoid(v[:, ch:])
    conv = cb_ref[...]
    for k in range(CONF_W):
        conv = conv + cw_ref[k:k + 1, :] * us_ref[pl.ds(32 - (CONF_W - 1) + k, lc), :]
    mu = jnp.mean(conv, axis=-1, keepdims=True)
    d = conv - mu
    var = jnp.mean(d * d, axis=-1, keepdims=True)
    o_ref[...] = _silu(d * lax.rsqrt(var + LN_EPS) * lg_ref[...] + lb_ref[...]).astype(o_ref.dtype)
    cs_ref[0] = us_ref[pl.ds(t_valid, 32), :]
    us_ref[pl.ds(0, 32), :] = us_ref[pl.ds(lc, 32), :]


def _conformer(x, prev32, glu_b, conv_w, conv_b, ln_g, ln_b, ns, lc, t_valid):
    m = x.shape[0]
    ch = x.shape[1] // 2
    nc = m // ns // lc
    row = lambda s, c: (s * nc + c, 0)
    seq = lambda s, c: (s, 0, 0)
    cst = lambda s, c: (0, 0)
    cw = jnp.concatenate([conv_w, jnp.zeros((32 - CONF_W, ch), f32)], axis=0)
    return pl.pallas_call(
        functools.partial(_conf_kernel, lc=lc, t_valid=t_valid),
        grid=(ns, nc),
        in_specs=[pl.BlockSpec((lc, 2 * ch), row), pl.BlockSpec((1, 32, ch), seq), pl.BlockSpec((1, 2 * ch), cst),
                  pl.BlockSpec((32, ch), cst), pl.BlockSpec((1, ch), cst), pl.BlockSpec((1, ch), cst),
                  pl.BlockSpec((1, ch), cst)],
        out_specs=[pl.BlockSpec((lc, ch), row), pl.BlockSpec((1, 32, ch), seq)],
        out_shape=[jax.ShapeDtypeStruct((m, ch), bf16), jax.ShapeDtypeStruct((ns, 32, ch), f32)],
        scratch_shapes=[pltpu.VMEM((lc + 32, ch), f32)],
        compiler_params=_cparams("parallel", "arbitrary"),
        name="conformer",
    )(x, prev32, glu_b.reshape(1, -1), cw, conv_b.reshape(1, -1), ln_g.reshape(1, -1), ln_b.reshape(1, -1))


def _merge_kernel(x_ref, g_ref, wg_ref, bg_ref, b0_ref, b1_ref, b2_ref, b3_ref, wb_ref, wo_ref, o_ref):
    x = x_ref[...]
    d = x.shape[1]
    h = (x * lax.rsqrt(jnp.mean(x * x, axis=-1, keepdims=True) + NORM_EPS) * g_ref[...]).astype(bf16)
    m = jnp.zeros(x.shape, f32)
    for i, b_ref in enumerate((b0_ref, b1_ref, b2_ref, b3_ref)):
        gate = _sigmoid(_dot(h, wg_ref[:, i * d:(i + 1) * d]) + bg_ref[:, i * d:(i + 1) * d])
        m = m + gate * _dot(b_ref[...], wb_ref[i])
    o_ref[...] = x + _dot(m.astype(bf16), wo_ref[...])


def _merge(x, g, wg, bg, branches, wb, wo, tm):
    m, d = x.shape
    row = lambda i: (i, 0)
    full = lambda a: pl.BlockSpec(a.shape, lambda i: (0,) * a.ndim)
    return pl.pallas_call(
        _merge_kernel,
        grid=(m // tm,),
        in_specs=[pl.BlockSpec((tm, d), row), full(g), full(wg), full(bg)]
        + [pl.BlockSpec((tm, b.shape[1]), row) for b in branches] + [full(wb), full(wo)],
        out_specs=pl.BlockSpec((tm, d), row),
        out_shape=jax.ShapeDtypeStruct((m, d), f32),
        compiler_params=_cparams("parallel"),
        name="merge",
    )(x, g, wg, bg, *branches, wb, wo)


def _ffn_kernel(x_ref, g_ref, wg_ref, wu_ref, wd_ref, o_ref, h_ref, acc_ref):
    f = pl.program_id(1)

    @pl.when(f == 0)
    def _():
        x = x_ref[...]
        h_ref[...] = (x * lax.rsqrt(jnp.mean(x * x, axis=-1, keepdims=True) + NORM_EPS) * g_ref[...]).astype(bf16)
        acc_ref[...] = jnp.zeros_like(acc_ref)

    h = h_ref[...]
    a = _silu(_dot(h, wg_ref[...])) * _dot(h, wu_ref[...])
    acc_ref[...] += _dot(a.astype(bf16), wd_ref[...])

    @pl.when(f == pl.num_programs(1) - 1)
    def _():
        o_ref[...] = x_ref[...] + acc_ref[...]


def _ffn(x, g, wg, wu, wd, tm, tf):
    m, d = x.shape
    nf = wg.shape[1] // tf
    return pl.pallas_call(
        _ffn_kernel,
        grid=(m // tm, nf),
        in_specs=[pl.BlockSpec((tm, d), lambda i, f: (i, 0)), pl.BlockSpec((1, d), lambda i, f: (0, 0)),
                  pl.BlockSpec((d, tf), lambda i, f: (0, f)), pl.BlockSpec((d, tf), lambda i, f: (0, f)),
                  pl.BlockSpec((tf, d), lambda i, f: (f, 0))],
        out_specs=pl.BlockSpec((tm, d), lambda i, f: (i, 0)),
        out_shape=jax.ShapeDtypeStruct((m, d), f32),
        scratch_shapes=[pltpu.VMEM((tm, d), bf16), pltpu.VMEM((tm, d), f32)],
        compiler_params=_cparams("parallel", "arbitrary"),
        name="ffn",
    )(x, g, wg, wu, wd)


def _route_kernel(x_ref, g_ref, wr_ref, br_ref, h_ref, r_ref):
    x = x_ref[...]
    hn = x * lax.rsqrt(jnp.mean(x * x, axis=-1, keepdims=True) + NORM_EPS) * g_ref[...]
    h_ref[...] = hn.astype(bf16)
    lane = _iota((1, LANES), 1)
    lanef = lane.astype(f32)
    logits = jnp.where(lane < N_EXPERTS, _dot_hi(hn, wr_ref[...]) + br_ref[...], -BIG)
    l1 = jnp.max(logits, axis=-1, keepdims=True)
    e1 = jnp.min(jnp.where(logits == l1, lanef, float(LANES)), axis=-1, keepdims=True)
    rest = jnp.where(lanef == e1, -BIG, logits)
    l2 = jnp.max(rest, axis=-1, keepdims=True)
    e2 = jnp.min(jnp.where(rest == l2, lanef, float(LANES)), axis=-1, keepdims=True)
    ex = jnp.exp(l2 - l1)
    w1 = 1.0 / (1.0 + ex)
    w2 = ex / (1.0 + ex)
    r_ref[...] = jnp.where(lane == 0, e1, jnp.where(lane == 1, e2, jnp.where(lane == 2, w1, jnp.where(lane == 3, w2, 0.0))))


def _route(x, g, w_router, b_router, tm):
    m, d = x.shape
    wr = jnp.zeros((d, LANES), f32).at[:, :N_EXPERTS].set(w_router)
    br = jnp.zeros((1, LANES), f32).at[0, :N_EXPERTS].set(b_router)
    return pl.pallas_call(
        _route_kernel,
        grid=(m // tm,),
        in_specs=[pl.BlockSpec((tm, d), lambda i: (i, 0)), pl.BlockSpec((1, d), lambda i: (0, 0)),
                  pl.BlockSpec((d, LANES), lambda i: (0, 0)), pl.BlockSpec((1, LANES), lambda i: (0, 0))],
        out_specs=[pl.BlockSpec((tm, d), lambda i: (i, 0)), pl.BlockSpec((tm, LANES), lambda i: (i, 0))],
        out_shape=[jax.ShapeDtypeStruct((m, d), bf16), jax.ShapeDtypeStruct((m, LANES), f32)],
        compiler_params=_cparams("parallel"),
        name="moe_route",
    )(x, g, wr, br)


def _expert_kernel(be_ref, x_ref, wg_ref, wu_ref, wd_ref, o_ref, acc_ref):
    del be_ref
    f = pl.program_id(1)

    @pl.when(f == 0)
    def _():
        acc_ref[...] = jnp.zeros_like(acc_ref)

    x = x_ref[...]
    a = _silu(_dot(x, wg_ref[0])) * _dot(x, wu_ref[0])
    acc_ref[...] += _dot(a.astype(bf16), wd_ref[0])

    @pl.when(f == pl.num_programs(1) - 1)
    def _():
        o_ref[...] = acc_ref[...]


def _experts(xb, blk_e, wg, wu, wd, bm, tf):
    r, d = xb.shape
    nf = wg.shape[2] // tf
    gs = pltpu.PrefetchScalarGridSpec(
        num_scalar_prefetch=1, grid=(r // bm, nf),
        in_specs=[pl.BlockSpec((bm, d), lambda b, f, e: (b, 0)),
                  pl.BlockSpec((1, d, tf), lambda b, f, e: (e[b], 0, f)),
                  pl.BlockSpec((1, d, tf), lambda b, f, e: (e[b], 0, f)),
                  pl.BlockSpec((1, tf, d), lambda b, f, e: (e[b], f, 0))],
        out_specs=pl.BlockSpec((bm, d), lambda b, f, e: (b, 0)),
        scratch_shapes=[pltpu.VMEM((bm, d), f32)])
    return pl.pallas_call(
        _expert_kernel, grid_spec=gs,
        out_shape=jax.ShapeDtypeStruct((r, d), f32),
        compiler_params=_cparams("parallel", "arbitrary"),
        name="moe_experts",
    )(blk_e, xb, wg, wu, wd)


def _moe(x, g, w_router, b_router, wg, wu, wd, tm, bm, tf):
    m, d = x.shape
    hb, r = _route(x, g, w_router, b_router, tm)
    top_e = r[:, 0:TOP_K].astype(i32)
    top_w = r[:, TOP_K:2 * TOP_K]
    n_slot = m * TOP_K
    slot_e = top_e.reshape(-1)
    order = jnp.argsort(slot_e)
    e_sorted = slot_e[order]
    counts = jnp.sum(slot_e[:, None] == jnp.arange(N_EXPERTS)[None, :], axis=0).astype(i32)
    padded = (counts + bm - 1) // bm * bm
    pad_end = jnp.cumsum(padded)
    dest_sorted = (pad_end - padded)[e_sorted] + jnp.arange(n_slot) - (jnp.cumsum(counts) - counts)[e_sorted]
    n_blk = -(-n_slot // bm) + N_EXPERTS
    buf_tok = jnp.zeros((n_blk * bm,), i32).at[dest_sorted].set((order // TOP_K).astype(i32))
    dest = jnp.zeros((n_slot,), i32).at[order].set(dest_sorted.astype(i32))
    blk_e = jnp.minimum(jnp.sum(pad_end[None, :] <= (jnp.arange(n_blk) * bm)[:, None], axis=1), N_EXPERTS - 1).astype(i32)
    yb = _experts(hb[buf_tok], blk_e, wg, wu, wd, bm, tf)
    ys = yb[dest].reshape(m, TOP_K, d) * top_w[:, :, None]
    return x + (ys[:, 0] + ys[:, 1])


def _tile(m, pref):
    for t in pref:
        if m % t == 0:
            return t
    return m


def _prep_weights(W, l):
    d = W["w_in"].shape[1]
    w_in = W["w_in"][l]
    offs = [0]
    for wd in (512, 768, 24, 512, 1024, 8, 1536, 1024, 4 * d):
        offs.append(offs[-1] + wd)
    seg = lambda i: w_in[:, offs[i]:offs[i + 1]]
    nsa_q, nsa_kv, nsa_g, ssd_z, ssd_xbc, ssd_dt, sb_qkv, conf_in, br_gate = (seg(i) for i in range(9))
    misc = jnp.concatenate([nsa_g, ssd_dt, jnp.zeros((d, LANES - 32), f32)], axis=1)
    w_tok = jnp.concatenate([conf_in, ssd_xbc, sb_qkv[:, :512] * (HEAD_DIM ** -0.5), sb_qkv[:, 512:], sb_qkv[:, 512:],
                             nsa_q, ssd_z, nsa_kv, misc], axis=1).astype(bf16)
    P = dict(w_tok=w_tok, wg=br_gate.astype(bf16), bg=W["b_gate"][l].reshape(1, -1),
             wb=W["w_branch"][l].astype(bf16), wo=W["w_out"][l].astype(bf16),
             g_mix=W["g_mix"][l].reshape(1, -1), g_ffn=W["g_ffn"][l].reshape(1, -1))
    P["cmp"] = _cmp_weights(W["nsa_cmp_pe"][l], W["nsa_cmp_w1"][l], W["nsa_cmp_w2"][l])
    return P


TOK_WIDTHS = (1024, 1024, 512, 1024, 1024, 512, 512, 768, 128)
TOK_DTYPES = (f32, f32, bf16, f32, bf16, f32, f32, f32, f32)


def _token_mixers(x2d, l, W, P, ns, t_new, pos0, past):
    m = x2d.shape[0]
    t_rows = m // ns
    decode = past is not None
    tm = _tile(m, (256, 128))
    conf, xbc, sbq, sbkv, sbkvb, nsaq, ssz, nsakv, misc = _inproj(x2d, P["g_mix"], P["w_tok"], TOK_WIDTHS, TOK_DTYPES, tm)

    pos = pos0 + jnp.arange(t_rows)
    tp = _tile(t_rows, (256, 128, 8))
    qn, rows, win, cmpk, cmpv, selkv, winkv = _nsa_prep(nsaq, nsakv, _rope_tables(pos), W["nsa_q_gain"][l],
                                                        W["nsa_k_gain"][l], ns, tp)
    w1a, w1b, cvec, w2bd = P["cmp"]
    if not decode:
        nchunk = t_rows // CMP_STRIDE
        kvc = _nsa_compress(cmpk.reshape(ns * nchunk, 2048), cmpv.reshape(ns * nchunk, 2048), w1a, w1b, cvec, w2bd,
                            _tile(ns * nchunk, (512, 256, 128)))
        tq = 128
        o_nsa = _nsa_attn(qn, misc, kvc.reshape(ns, nchunk, 256), selkv.reshape(ns, t_rows, 256),
                          winkv.reshape(ns, t_rows, 256), ns, tq, 512, 0, 0, WINDOW + tq, True)
        nsa_win = win.reshape(ns, t_rows, 2, NSA_KV_HEADS, HEAD_DIM)[:, t_rows - min(WINDOW, t_rows):]
    else:
        pool, tbl, past_win, layer_off = past["nsa_pool"], past["tbl"], past["nsa_win"], past["nsa_off"]
        past_len = tbl.shape[1] * PAGE
        new_rows = jnp.zeros((ns, PAGE, 512), f32).at[:, :t_new].set(rows.reshape(ns, t_rows, 512)[:, :t_new])
        pg = 8
        ck, cv, skv = _page_copy(pool, tbl, new_rows, layer_off, pg)
        t_pad = ck.shape[1]
        nchunk = t_pad // CMP_STRIDE
        kvc = _nsa_compress(ck.reshape(ns * nchunk, 2048), cv.reshape(ns * nchunk, 2048), w1a, w1b, cvec, w2bd,
                            _tile(ns * nchunk, (1024, 512, 256, 128)))
        win_new = win.reshape(ns, t_rows, 2, NSA_KV_HEADS, HEAD_DIM)[:, :t_new]
        w_all = jnp.concatenate([past_win, win_new], axis=1)
        wl = w_all.shape[1]
        wl_pad = -(-wl // 8) * 8
        wkv = jnp.zeros((ns, wl_pad, 256), bf16).at[:, :wl].set(w_all.reshape(ns, wl, 256).astype(bf16))
        o_nsa = _nsa_attn(qn, misc, kvc.reshape(ns, nchunk, 256), skv, wkv, ns, t_rows, 512, past_len,
                          past_len - past_win.shape[1], wl_pad, False)
        nsa_win = w_all[:, t_new:]
    nsa_rows = rows.reshape(ns, t_rows, 4, NSA_KV_HEADS, HEAD_DIM)[:, :t_new]

    if not decode:
        prev8 = jnp.zeros((ns, 8, xbc.shape[1]), f32)
        h0 = jnp.zeros((ns, SSD_HEADS * SSD_P, SSD_N), f32)
        lc = min(SSD_CHUNK, t_rows)
    else:
        prev8 = jnp.zeros((ns, 8, xbc.shape[1]), f32).at[:, 8 - (SSD_CONV - 1):].set(past["ssd_conv"])
        h0 = past["ssm"].reshape(ns, SSD_HEADS * SSD_P, SSD_N)
        lc = t_rows
    o_ssd, h_new, cst = _ssd(xbc, ssz, misc, prev8, h0, W["ssd_conv_w"][l], W["ssd_conv_b"][l], W["ssd_dt_bias"][l],
                             W["ssd_a_log"][l], W["ssd_d"][l], W["ssd_norm_g"][l], ns, lc, t_new if decode else lc)
    ssm = h_new.reshape(ns, SSD_HEADS, SSD_P, SSD_N)
    ssd_conv = cst[:, 8 - (SSD_CONV - 1):]

    if not decode:
        o_sb = _sb_prefill(sbq, sbkvb.reshape(ns, t_rows, 1024), ns, _tile(t_rows, (256, 128)))
    else:
        q4 = sbq.reshape(ns, t_rows, SB_HEADS, HEAD_DIM).transpose(0, 2, 1, 3)
        eye = jnp.eye(SB_HEADS, dtype=bf16)
        qbd = (q4[:, :, :, None, :] * eye[None, :, None, :, None]).reshape(ns, SB_HEADS * t_rows, SB_HEADS * HEAD_DIM)
        kvn = jnp.zeros((ns, PAGE, 1024), bf16).at[:, :t_new].set(sbkvb.reshape(ns, t_rows, 1024)[:, :t_new])
        o_sb = _sb_decode(qbd, kvn[:, :, :512], kvn[:, :, 512:], past["sb_pool"], past["tbl"], past["sb_off"], 8, t_new)
        o_sb = o_sb.reshape(ns * t_rows, 512)
    sb_rows = sbkv.reshape(ns, t_rows, 2, SB_HEADS, HEAD_DIM)[:, :t_new]

    ch = conf.shape[1] // 2
    if not decode:
        prev32 = jnp.zeros((ns, 32, ch), f32)
        lcc = _tile(t_rows, (256, 128))
    else:
        prev32 = jnp.zeros((ns, 32, ch), f32).at[:, 32 - (CONF_W - 1):].set(past["conf_conv"])
        lcc = t_rows
    o_conf, ccs = _conformer(conf, prev32, W["conf_glu_b"][l], W["conf_conv_w"][l], W["conf_conv_b"][l],
                             W["conf_ln_g"][l], W["conf_ln_b"][l], ns, lcc, t_new if decode else lcc)
    conf_conv = ccs[:, 32 - (CONF_W - 1):]

    x_mid = _merge(x2d, P["g_mix"], P["wg"], P["bg"], (o_nsa, o_ssd, o_sb, o_conf), P["wb"], P["wo"], tm)
    return x_mid, (nsa_rows, nsa_win, sb_rows, ssm, ssd_conv, conf_conv)


def kernel(x_prompt, x_sample, cache_nsa_kv, cache_nsa_win, cache_sb_kv, state_ssm, state_ssd_conv, state_conf_conv, page_table, g_mix, w_in, nsa_q_gain, nsa_k_gain, nsa_cmp_pe, nsa_cmp_w1, nsa_cmp_w2, ssd_conv_w, ssd_conv_b, ssd_dt_bias, ssd_a_log, ssd_d, ssd_norm_g, conf_glu_b, conf_conv_w, conf_conv_b, conf_ln_g, conf_ln_b, w_branch, b_gate, w_out, g_ffn, ff_w_gate, ff_w_up, ff_w_down, moe_w_router, moe_b_router, moe_w_gate, moe_w_up, moe_w_down):
    W = dict(g_mix=g_mix, w_in=w_in, nsa_q_gain=nsa_q_gain, nsa_k_gain=nsa_k_gain, nsa_cmp_pe=nsa_cmp_pe,
             nsa_cmp_w1=nsa_cmp_w1, nsa_cmp_w2=nsa_cmp_w2, ssd_conv_w=ssd_conv_w, ssd_conv_b=ssd_conv_b,
             ssd_dt_bias=ssd_dt_bias, ssd_a_log=ssd_a_log, ssd_d=ssd_d, ssd_norm_g=ssd_norm_g, conf_glu_b=conf_glu_b,
             conf_conv_w=conf_conv_w, conf_conv_b=conf_conv_b, conf_ln_g=conf_ln_g, conf_ln_b=conf_ln_b,
             w_branch=w_branch, b_gate=b_gate, w_out=w_out, g_ffn=g_ffn)
    depth = w_in.shape[0]
    nb, t, d = x_prompt.shape
    ns, ts, _ = x_sample.shape
    n_phys = cache_nsa_kv.shape[1]
    past_len = page_table.shape[1] * PAGE
    ts_pad = -(-ts // 8) * 8
    nsa_pool = cache_nsa_kv.reshape(depth * n_phys, PAGE, 4 * NSA_KV_HEADS * HEAD_DIM).astype(bf16)
    sb_pool = cache_sb_kv.reshape(depth * n_phys, PAGE, 2 * SB_HEADS * HEAD_DIM).astype(bf16)

    xp = x_prompt.reshape(nb * t, d)
    xs = jnp.zeros((ns, ts_pad, d), f32).at[:, :ts].set(x_sample).reshape(ns * ts_pad, d)
    st_p, st_s = [], []
    for l in range(depth):
        P = _prep_weights(W, l)
        xp, sp = _token_mixers(xp, l, W, P, nb, t, 0, None)
        past = dict(nsa_pool=nsa_pool, sb_pool=sb_pool, tbl=page_table, nsa_off=l * n_phys, sb_off=l * n_phys,
                    nsa_win=cache_nsa_win[l], ssm=state_ssm[l], ssd_conv=state_ssd_conv[l], conf_conv=state_conf_conv[l])
        xs, ss = _token_mixers(xs, l, W, P, ns, ts, past_len, past)
        i = l // 2
        if l % 2 == 0:
            wg, wu, wd = ff_w_gate[i].astype(bf16), ff_w_up[i].astype(bf16), ff_w_down[i].astype(bf16)
            tf = _tile(wg.shape[1], (1408, 512, 256, 128))
            xp = _ffn(xp, P["g_ffn"], wg, wu, wd, _tile(xp.shape[0], (512, 256, 128)), tf)
            xs = _ffn(xs, P["g_ffn"], wg, wu, wd, _tile(xs.shape[0], (512, 256, 128)), tf)
        else:
            wg, wu, wd = moe_w_gate[i].astype(bf16), moe_w_up[i].astype(bf16), moe_w_down[i].astype(bf16)
            tf = _tile(wg.shape[2], (1792, 512, 256, 128))
            xp = _moe(xp, P["g_ffn"], moe_w_router[i], moe_b_router[i], wg, wu, wd, _tile(xp.shape[0], (512, 256, 128)), 512, tf)
            xs = _moe(xs, P["g_ffn"], moe_w_router[i], moe_b_router[i], wg, wu, wd, _tile(xs.shape[0], (512, 256, 128)), 128, tf)
        st_p.append(sp)
        st_s.append(ss)
    y_prompt = xp.reshape(nb, t, d)
    y_sample = xs.reshape(ns, ts_pad, d)[:, :ts]
    nsa_kv_p, nsa_win_p, sb_kv_p, ssm_p, ssd_conv_p, conf_conv_p = [jnp.stack(s, axis=0) for s in zip(*st_p)]
    nsa_kv_s, nsa_win_s, sb_kv_s, ssm_s, ssd_conv_s, conf_conv_s = [jnp.stack(s, axis=0) for s in zip(*st_s)]
    return (y_prompt, y_sample, nsa_kv_p, nsa_kv_s, nsa_win_p, nsa_win_s, sb_kv_p, sb_kv_s,
            ssm_p, ssm_s, ssd_conv_p, ssd_conv_s, conf_conv_p, conf_conv_s)
```
